```python
import jax, jax.numpy as jnp
from jax import lax
import numpy as np

D_MODEL = 2048
BATCH = 4
SEQ = 4096
DEPTH = 1

GRID_W = 64
CTX_LEN = 256
N_MOD = 6
EPS = 1e-6
POOL_WINDOWS = (2, 4, 8, 16)
POOL_GROUPS = 4
POOL_GROUP_DIM = D_MODEL // 8
POOL_DIM = POOL_GROUPS * POOL_GROUP_DIM
MLA_HEADS = D_MODEL // 128
Q_LORA = D_MODEL // 4
KV_LORA = D_MODEL // 4
QK_NOPE = 128
QK_ROPE = 64
V_HEAD = 128
QK_HEAD = QK_NOPE + QK_ROPE
ROPE_FREQS = QK_ROPE // 4
ROPE_BASE = 10000.0
Q_BLOCK = 128
ATTN_SCALE = QK_HEAD ** -0.5
N_BRANCH = 2
IN_DIM = POOL_DIM + Q_LORA + KV_LORA + QK_ROPE + N_BRANCH * D_MODEL
SPLITS = (POOL_DIM, POOL_DIM + Q_LORA, POOL_DIM + Q_LORA + KV_LORA, POOL_DIM + Q_LORA + KV_LORA + QK_ROPE)
N_GROUPS = 4
EXPERTS_PER_GROUP = 8
N_EXPERTS = N_GROUPS * EXPERTS_PER_GROUP
TOP_K = 2
D_EXPERT = D_MODEL // 4

kernel_name = "hybrid_pool_mla_hmoe_diffusion_block"


def rms_norm(x, g):
    xf = x.astype(jnp.float32)
    y = xf * lax.rsqrt(jnp.mean(xf * xf, axis=-1, keepdims=True) + EPS)
    return (y * g.astype(jnp.float32)).astype(x.dtype)


def modulate(x, shift, scale):
    return x * (1 + scale) + shift


def axial_rope_tables(L):
    rows = L // GRID_W
    row = jnp.repeat(jnp.arange(rows, dtype=jnp.float32), GRID_W)
    col = jnp.tile(jnp.arange(GRID_W, dtype=jnp.float32), rows)
    freqs = 1.0 / (ROPE_BASE ** (jnp.arange(ROPE_FREQS, dtype=jnp.float32) / ROPE_FREQS))
    ang_r = row[:, None] * freqs[None, :]
    ang_c = col[:, None] * freqs[None, :]
    ang = jnp.concatenate([ang_r, ang_r, ang_c, ang_c], axis=-1)
    return jnp.cos(ang), jnp.sin(ang)


def apply_rope(x, cos, sin):
    xs = x.reshape(x.shape[:-1] + (2, 2, ROPE_FREQS))
    rot = jnp.stack([-xs[..., 1, :], xs[..., 0, :]], axis=-2).reshape(x.shape)
    return (x * cos + rot * sin).astype(x.dtype)


def pool_diff(u, window):
    B, L, C = u.shape
    cs = jnp.concatenate([jnp.zeros((B, 1, C), jnp.float32), jnp.cumsum(u.astype(jnp.float32), axis=1)], axis=1)
    t = jnp.arange(L)
    hi = jnp.minimum(t + window // 2, L)
    lo = jnp.maximum(t - window // 2, 0)
    cnt = (hi - lo).astype(jnp.float32)[None, :, None]
    mean = (cs[:, hi] - cs[:, lo]) / cnt
    return mean.astype(u.dtype) - u


def pool_branch(u, pool_w, pool_scale):
    B, L, _ = u.shape
    ug = u.reshape(B, L, POOL_GROUPS, POOL_GROUP_DIM)
    pooled = jnp.stack([pool_diff(ug[:, :, g], w) for g, w in enumerate(POOL_WINDOWS)], axis=2)
    mixed = jnp.einsum('blgc,gcd->blgd', pooled, pool_w)
    return mixed.reshape(B, L, POOL_DIM) * pool_scale


def mla_queries(q_c, q_norm, w_uq, cos, sin):
    B, L, _ = q_c.shape
    q = (rms_norm(q_c, q_norm) @ w_uq).reshape(B, L, MLA_HEADS, QK_HEAD)
    if cos is not None:
        q = jnp.concatenate([q[..., :QK_NOPE], apply_rope(q[..., QK_NOPE:], cos[:, None, :], sin[:, None, :])], axis=-1)
    return q


def mla_keys_values(kv_c, k_r, kv_norm, w_ukv, cos, sin):
    B, L, _ = kv_c.shape
    kv = (rms_norm(kv_c, kv_norm) @ w_ukv).reshape(B, L, MLA_HEADS, QK_NOPE + V_HEAD)
    k_nope, v = kv[..., :QK_NOPE], kv[..., QK_NOPE:]
    if cos is not None:
        k_r = apply_rope(k_r, cos, sin)
    k = jnp.concatenate([k_nope, jnp.broadcast_to(k_r[:, :, None, :], (B, L, MLA_HEADS, QK_ROPE))], axis=-1)
    return k, v


def attend(q, k, v):
    s = jnp.einsum('bqhd,bkhd->bhqk', q, k).astype(jnp.float32) * ATTN_SCALE
    p = jax.nn.softmax(s, axis=-1).astype(v.dtype)
    return jnp.einsum('bhqk,bkhd->bqhd', p, v)


def latent_attention(q, k_lat, v_lat, k_ctx, v_ctx):
    k = jnp.concatenate([k_lat, k_ctx], axis=1)
    v = jnp.concatenate([v_lat, v_ctx], axis=1)
    B, L, H, Dq = q.shape
    nb = L // Q_BLOCK
    qb = q.reshape(B, nb, Q_BLOCK, H, Dq).transpose(1, 0, 2, 3, 4)
    o = lax.map(lambda qi: attend(qi, k, v), qb)
    return o.transpose(1, 0, 2, 3, 4).reshape(B, L, H * V_HEAD)


def merge_branches(pool_in, attn_out, gates, pool_w, pool_scale, w_pool_up, w_attn_o, w_out):
    pool_out = pool_branch(pool_in, pool_w, pool_scale)
    g_pool, g_attn = jnp.split(gates, N_BRANCH, axis=-1)
    merged = jax.nn.sigmoid(g_pool) * (pool_out @ w_pool_up) + jax.nn.sigmoid(g_attn) * (attn_out @ w_attn_o)
    return merged @ w_out


def hier_moe(h, wg, bg, we, be, w_gate, w_up, w_down):
    B, L, D = h.shape
    t = h.reshape(-1, D)
    pg = jax.nn.softmax((t @ wg + bg).astype(jnp.float32), axis=-1)
    pg_top, g_idx = lax.top_k(pg, 1)
    le = (t @ we + be).astype(jnp.float32).reshape(-1, N_GROUPS, EXPERTS_PER_GROUP)
    le_sel = jnp.einsum('tg,tge->te', jax.nn.one_hot(g_idx[:, 0], N_GROUPS, dtype=jnp.float32), le)
    pe = jax.nn.softmax(le_sel, axis=-1)
    pe_top, e_idx = lax.top_k(pe, TOP_K)
    w = pg_top * pe_top / jnp.sum(pe_top, axis=-1, keepdims=True)
    eid = g_idx * EXPERTS_PER_GROUP + e_idx
    combine = jnp.einsum('tk,tke->te', w, jax.nn.one_hot(eid, N_EXPERTS, dtype=jnp.float32)).astype(h.dtype)
    y = jnp.zeros_like(t)
    for e in range(N_EXPERTS):
        y = y + combine[:, e:e + 1] * ((jax.nn.silu(t @ w_gate[e]) * (t @ w_up[e])) @ w_down[e])
    return y.reshape(B, L, D)


def setup_inputs(seed: int = 0) -> dict:
    key = jax.random.key(seed)
    ks = jax.random.split(key, 32)
    f = jnp.float32

    def nrm(k, shape, scale):
        return jax.random.normal(k, shape, f) * scale

    def gain(k, shape):
        return 1.0 + 0.02 * jax.random.normal(k, shape, f)

    Dp = DEPTH
    return {
        "x": nrm(ks[0], (BATCH, SEQ, D_MODEL), 1.0),
        "c": nrm(ks[1], (BATCH, D_MODEL), 1.0),
        "ctx": nrm(ks[2], (BATCH, CTX_LEN, D_MODEL), 1.0),
        "c_ctx": nrm(ks[3], (D_MODEL,), 1.0),
        "w_mod": nrm(ks[4], (Dp, D_MODEL, N_MOD * D_MODEL), 0.5 * D_MODEL ** -0.5),
        "b_mod": nrm(ks[5], (Dp, N_MOD * D_MODEL), 0.02),
        "norm_pre_mix": gain(ks[6], (Dp, D_MODEL)),
        "norm_post_mix": gain(ks[7], (Dp, D_MODEL)),
        "norm_pre_ffn": gain(ks[8], (Dp, D_MODEL)),
        "norm_post_ffn": gain(ks[9], (Dp, D_MODEL)),
        "w_in": nrm(ks[10], (Dp, D_MODEL, IN_DIM), D_MODEL ** -0.5),
        "pool_w": nrm(ks[11], (Dp, POOL_GROUPS, POOL_GROUP_DIM, POOL_GROUP_DIM), POOL_GROUP_DIM ** -0.5),
        "pool_scale": 1.0 + 0.1 * jax.random.normal(ks[12], (Dp, POOL_DIM), f),
        "w_pool_up": nrm(ks[13], (Dp, POOL_DIM, D_MODEL), POOL_DIM ** -0.5),
        "q_norm": gain(ks[14], (Dp, Q_LORA)),
        "w_uq": nrm(ks[15], (Dp, Q_LORA, MLA_HEADS * QK_HEAD), Q_LORA ** -0.5),
        "kv_norm": gain(ks[16], (Dp, KV_LORA)),
        "w_ukv": nrm(ks[17], (Dp, KV_LORA, MLA_HEADS * (QK_NOPE + V_HEAD)), KV_LORA ** -0.5),
        "w_attn_o": nrm(ks[18], (Dp, MLA_HEADS * V_HEAD, D_MODEL), (MLA_HEADS * V_HEAD) ** -0.5),
        "w_out": nrm(ks[19], (Dp, D_MODEL, D_MODEL), D_MODEL ** -0.5),
        "router_group_w": nrm(ks[20], (Dp, D_MODEL, N_GROUPS), D_MODEL ** -0.5),
        "router_group_b": nrm(ks[21], (Dp, N_GROUPS), 0.01),
        "router_expert_w": nrm(ks[22], (Dp, D_MODEL, N_EXPERTS), D_MODEL ** -0.5),
        "router_expert_b": nrm(ks[23], (Dp, N_EXPERTS), 0.01),
        "w_gate": nrm(ks[24], (Dp, N_EXPERTS, D_MODEL, D_EXPERT), D_MODEL ** -0.5),
        "w_up": nrm(ks[25], (Dp, N_EXPERTS, D_MODEL, D_EXPERT), D_MODEL ** -0.5),
        "w_down": nrm(ks[26], (Dp, N_EXPERTS, D_EXPERT, D_MODEL), D_EXPERT ** -0.5),
    }


def reference(x, c, ctx, c_ctx, w_mod, b_mod, norm_pre_mix, norm_post_mix, norm_pre_ffn, norm_post_ffn,
              w_in, pool_w, pool_scale, w_pool_up, q_norm, w_uq, kv_norm, w_ukv, w_attn_o, w_out,
              router_group_w, router_group_b, router_expert_w, router_expert_b, w_gate, w_up, w_down):
    L = x.shape[1]
    cos, sin = axial_rope_tables(L)
    ctx_s = ctx
    for i in range(DEPTH):
        last = i + 1 == DEPTH
        mod_x = (jax.nn.silu(c) @ w_mod[i] + b_mod[i])[:, None, :]
        mod_c = (jax.nn.silu(c_ctx) @ w_mod[i] + b_mod[i])[None, None, :]
        sh1, sc1, g1, sh2, sc2, g2 = jnp.split(mod_x, N_MOD, axis=-1)
        sh1c, sc1c, g1c, sh2c, sc2c, g2c = jnp.split(mod_c, N_MOD, axis=-1)

        h_x = modulate(rms_norm(x, norm_pre_mix[i]), sh1, sc1)
        h_c = modulate(rms_norm(ctx_s, norm_pre_mix[i]), sh1c, sc1c)
        pool_x, qc_x, kvc_x, kr_x, gate_x = jnp.split(h_x @ w_in[i], SPLITS, axis=-1)
        pool_c, qc_c, kvc_c, kr_c, gate_c = jnp.split(h_c @ w_in[i], SPLITS, axis=-1)
        k_c, v_c = mla_keys_values(kvc_c, kr_c, kv_norm[i], w_ukv[i], None, None)
        q_x = mla_queries(qc_x, q_norm[i], w_uq[i], cos, sin)
        k_x, v_x = mla_keys_values(kvc_x, kr_x, kv_norm[i], w_ukv[i], cos, sin)
        attn_x = latent_attention(q_x, k_x, v_x, k_c, v_c)
        y_x = merge_branches(pool_x, attn_x, gate_x, pool_w[i], pool_scale[i], w_pool_up[i], w_attn_o[i], w_out[i])
        x = x + g1 * rms_norm(y_x, norm_post_mix[i])
        if not last:
            q_c = mla_queries(qc_c, q_norm[i], w_uq[i], None, None)
            attn_c = attend(q_c, k_c, v_c).reshape(ctx_s.shape[0], ctx_s.shape[1], MLA_HEADS * V_HEAD)
            y_c = merge_branches(pool_c, attn_c, gate_c, pool_w[i], pool_scale[i], w_pool_up[i], w_attn_o[i], w_out[i])
            ctx_s = ctx_s + g1c * rms_norm(y_c, norm_post_mix[i])

        h2 = modulate(rms_norm(x, norm_pre_ffn[i]), sh2, sc2)
        y2 = hier_moe(h2, router_group_w[i], router_group_b[i], router_expert_w[i], router_expert_b[i],
                      w_gate[i], w_up[i], w_down[i])
        x = x + g2 * rms_norm(y2, norm_post_ffn[i])
        if not last:
            h2c = modulate(rms_norm(ctx_s, norm_pre_ffn[i]), sh2c, sc2c)
            y2c = hier_moe(h2c, router_group_w[i], router_group_b[i], router_expert_w[i], router_expert_b[i],
                           w_gate[i], w_up[i], w_down[i])
            ctx_s = ctx_s + g2c * rms_norm(y2c, norm_post_ffn[i])
    return x
```

```python
import functools

import jax
import jax.numpy as jnp
import numpy as np
from jax import lax
from jax.experimental import pallas as pl
from jax.experimental.pallas import tpu as pltpu

F32 = jnp.float32
BF16 = jnp.bfloat16

D_MODEL = 2048
N_MOD = 6
EPS = 1e-6
GRID_W = 64
POOL_WINDOWS = (2, 4, 8, 16)
POOL_GROUPS = 4
POOL_GROUP_DIM = D_MODEL // 8
POOL_DIM = POOL_GROUPS * POOL_GROUP_DIM
MLA_HEADS = D_MODEL // 128
Q_LORA = D_MODEL // 4
KV_LORA = D_MODEL // 4
QK_NOPE = 128
QK_ROPE = 64
V_HEAD = 128
QK_HEAD = QK_NOPE + QK_ROPE
ROPE_FREQS = QK_ROPE // 4
ROPE_BASE = 10000.0
ATTN_SCALE = QK_HEAD ** -0.5
N_GROUPS = 4
EXPERTS_PER_GROUP = 8
N_EXPERTS = N_GROUPS * EXPERTS_PER_GROUP
D_EXPERT = D_MODEL // 4

LANES = 128
HEAD_PAD = 2 * LANES
HALO = 16
V7X_VMEM_BYTES = 64 * 1024 * 1024
VMEM_LIMIT = 56 * 1024 * 1024

COL_GATE_POOL = 0
COL_GATE_ATTN = D_MODEL
COL_POOL = 2 * D_MODEL
COL_QC = COL_POOL + POOL_DIM
COL_KVC = COL_QC + Q_LORA
COL_KRR = COL_KVC + KV_LORA
IN_COLS = COL_KRR + LANES
IN_COLS_PAD = 6400
IN_BLOCK_N = 1280


def _params(*sem):
    return pltpu.CompilerParams(dimension_semantics=sem, vmem_limit_bytes=VMEM_LIMIT)


def _sigmoid(v):
    return 1.0 / (1.0 + jnp.exp(-v))


def _rms(v, g):
    v = v.astype(F32)
    return v * lax.rsqrt(jnp.mean(v * v, axis=-1, keepdims=True) + EPS) * g


def _const_spec(shape):
    nd = len(shape)
    return pl.BlockSpec(shape, lambda *_: (0,) * nd, pipeline_mode=pl.Buffered(1))


def _mod_kernel(c_ref, w_ref, b_ref, o_ref):
    a = c_ref[...]
    a = a * _sigmoid(a)
    o_ref[...] = jnp.dot(a.astype(BF16), w_ref[...].astype(BF16), preferred_element_type=F32) + b_ref[...]


def _modulation(cvec, w_mod, b_mod):
    rows = cvec.shape[0]
    n = w_mod.shape[1]
    bn = 1536
    return pl.pallas_call(
        _mod_kernel,
        out_shape=jax.ShapeDtypeStruct((rows, n), F32),
        grid=(n // bn,),
        in_specs=[pl.BlockSpec((rows, D_MODEL), lambda j: (0, 0)),
                  pl.BlockSpec((D_MODEL, bn), lambda j: (0, j)),
                  pl.BlockSpec((1, bn), lambda j: (0, j))],
        out_specs=pl.BlockSpec((rows, bn), lambda j: (0, j)),
        compiler_params=_params("arbitrary"),
        name="mod",
    )(cvec, w_mod, b_mod.reshape(1, n))


def _inproj_kernel(x_ref, g_ref, sh_ref, sc_ref, w_ref, o_ref, h_ref):
    @pl.when(pl.program_id(2) == 0)
    def _():
        h = _rms(x_ref[0], g_ref[...]) * (1.0 + sc_ref[0]) + sh_ref[0]
        h_ref[...] = h.astype(BF16)

    o_ref[0] = jnp.dot(h_ref[...], w_ref[...], preferred_element_type=F32).astype(o_ref.dtype)


def _in_proj(x, gain, shift, scale, w, tm, bn):
    b, l, _ = x.shape
    n = w.shape[1]
    per_batch = shift.shape[0] == b
    mod_map = (lambda bi, i, j: (bi, 0, 0)) if per_batch else (lambda bi, i, j: (0, 0, 0))
    return pl.pallas_call(
        _inproj_kernel,
        out_shape=jax.ShapeDtypeStruct((b, l, n), BF16),
        grid=(b, l // tm, n // bn),
        in_specs=[pl.BlockSpec((1, tm, D_MODEL), lambda bi, i, j: (bi, i, 0)),
                  pl.BlockSpec((1, D_MODEL), lambda bi, i, j: (0, 0)),
                  pl.BlockSpec((1, 1, D_MODEL), mod_map),
                  pl.BlockSpec((1, 1, D_MODEL), mod_map),
                  pl.BlockSpec((D_MODEL, bn), lambda bi, i, j: (0, j))],
        out_specs=pl.BlockSpec((1, tm, bn), lambda bi, i, j: (bi, i, j)),
        scratch_shapes=[pltpu.VMEM((tm, D_MODEL), BF16)],
        compiler_params=_params("arbitrary", "arbitrary", "arbitrary"),
        name="in_proj",
    )(x, gain.reshape(1, D_MODEL), shift, scale, w)


def _mla_latent_kernel(qc_ref, kvc_ref, krr_ref, qn_ref, kvn_ref, wq_ref, wkv_ref, tq_ref, tk_ref,
                       q_out, k_out, v_out):
    qn = _rms(qc_ref[0], qn_ref[...]).astype(BF16)
    q = jnp.dot(qn, wq_ref[...], preferred_element_type=F32)
    tq = tq_ref[...]
    for h in range(MLA_HEADS):
        q_out[0, :, h * HEAD_PAD:(h + 1) * HEAD_PAD] = (q[:, h * HEAD_PAD:(h + 1) * HEAD_PAD] * tq).astype(BF16)
    kvn = _rms(kvc_ref[0], kvn_ref[...]).astype(BF16)
    kv = jnp.dot(kvn, wkv_ref[...], preferred_element_type=F32)
    u = krr_ref[0].astype(F32) * tk_ref[...]
    krot = (u + pltpu.roll(u, QK_ROPE, axis=1)).astype(BF16)
    for h in range(MLA_HEADS):
        k_out[0, :, h * HEAD_PAD:h * HEAD_PAD + QK_NOPE] = kv[:, h * QK_NOPE:(h + 1) * QK_NOPE].astype(BF16)
        k_out[0, :, h * HEAD_PAD + QK_NOPE:(h + 1) * HEAD_PAD] = krot
    v_out[0] = kv[:, MLA_HEADS * QK_NOPE:].astype(BF16)


def _mla_ctx_kernel(kvc_ref, krr_ref, kvn_ref, wkv_ref, k_out, v_out):
    kvn = _rms(kvc_ref[0], kvn_ref[...]).astype(BF16)
    kv = jnp.dot(kvn, wkv_ref[...], preferred_element_type=F32)
    krr = krr_ref[0]
    for h in range(MLA_HEADS):
        k_out[0, :, h * HEAD_PAD:h * HEAD_PAD + QK_NOPE] = kv[:, h * QK_NOPE:(h + 1) * QK_NOPE].astype(BF16)
        k_out[0, :, h * HEAD_PAD + QK_NOPE:(h + 1) * HEAD_PAD] = krr
    v_out[0] = kv[:, MLA_HEADS * QK_NOPE:].astype(BF16)


def _mla_latent(p1, q_norm, kv_norm, wq, wkv, tabq, tabk, tm):
    b, l, _ = p1.shape
    hq = MLA_HEADS * HEAD_PAD
    hv = MLA_HEADS * V_HEAD
    return pl.pallas_call(
        _mla_latent_kernel,
        out_shape=(jax.ShapeDtypeStruct((b, l, hq), BF16),
                   jax.ShapeDtypeStruct((b, l, hq), BF16),
                   jax.ShapeDtypeStruct((b, l, hv), BF16)),
        grid=(b, l // tm),
        in_specs=[pl.BlockSpec((1, tm, Q_LORA), lambda bi, i: (bi, i, COL_QC // Q_LORA)),
                  pl.BlockSpec((1, tm, KV_LORA), lambda bi, i: (bi, i, COL_KVC // KV_LORA)),
                  pl.BlockSpec((1, tm, LANES), lambda bi, i: (bi, i, COL_KRR // LANES)),
                  _const_spec((1, Q_LORA)),
                  _const_spec((1, KV_LORA)),
                  _const_spec((Q_LORA, hq)),
                  _const_spec((KV_LORA, hq)),
                  pl.BlockSpec((tm, HEAD_PAD), lambda bi, i: (i, 0)),
                  pl.BlockSpec((tm, LANES), lambda bi, i: (i, 0))],
        out_specs=(pl.BlockSpec((1, tm, hq), lambda bi, i: (bi, i, 0)),
                   pl.BlockSpec((1, tm, hq), lambda bi, i: (bi, i, 0)),
                   pl.BlockSpec((1, tm, hv), lambda bi, i: (bi, i, 0))),
        compiler_params=_params("arbitrary", "arbitrary"),
        name="mla_latent",
    )(p1, p1, p1, q_norm.reshape(1, Q_LORA), kv_norm.reshape(1, KV_LORA), wq, wkv, tabq, tabk)


def _mla_ctx(pc, kv_norm, wkv, tm):
    b, c, _ = pc.shape
    hq = MLA_HEADS * HEAD_PAD
    hv = MLA_HEADS * V_HEAD
    return pl.pallas_call(
        _mla_ctx_kernel,
        out_shape=(jax.ShapeDtypeStruct((b, c, hq), BF16),
                   jax.ShapeDtypeStruct((b, c, hv), BF16)),
        grid=(b, c // tm),
        in_specs=[pl.BlockSpec((1, tm, KV_LORA), lambda bi, i: (bi, i, 0)),
                  pl.BlockSpec((1, tm, LANES), lambda bi, i: (bi, i, KV_LORA // LANES)),
                  _const_spec((1, KV_LORA)),
                  _const_spec((KV_LORA, hq))],
        out_specs=(pl.BlockSpec((1, tm, hq), lambda bi, i: (bi, i, 0)),
                   pl.BlockSpec((1, tm, hv), lambda bi, i: (bi, i, 0))),
        compiler_params=_params("arbitrary", "arbitrary"),
        name="mla_ctx",
    )(pc, pc, kv_norm.reshape(1, KV_LORA), wkv)


def _attn_kernel(q_ref, kl_ref, vl_ref, kc_ref, vc_ref, o_ref):
    q = q_ref[0]
    nt = (((1,), (1,)), ((), ()))
    s1 = lax.dot_general(q, kl_ref[0], nt, preferred_element_type=F32)
    s2 = lax.dot_general(q, kc_ref[0], nt, preferred_element_type=F32)
    m = jnp.maximum(jnp.max(s1, axis=-1, keepdims=True), jnp.max(s2, axis=-1, keepdims=True))
    p1 = jnp.exp(s1 - m)
    p2 = jnp.exp(s2 - m)
    denom = jnp.sum(p1, axis=-1, keepdims=True) + jnp.sum(p2, axis=-1, keepdims=True)
    o = (jnp.dot(p1.astype(BF16), vl_ref[0], preferred_element_type=F32)
         + jnp.dot(p2.astype(BF16), vc_ref[0], preferred_element_type=F32))
    o_ref[0] = (o / denom).astype(o_ref.dtype)


def _attention(q, k_lat, v_lat, k_ctx, v_ctx, tq):
    b, l, _ = q.shape
    c = k_ctx.shape[1]
    return pl.pallas_call(
        _attn_kernel,
        out_shape=jax.ShapeDtypeStruct((b, l, MLA_HEADS * V_HEAD), BF16),
        grid=(b, MLA_HEADS, l // tq),
        in_specs=[pl.BlockSpec((1, tq, HEAD_PAD), lambda bi, h, i: (bi, i, h)),
                  pl.BlockSpec((1, l, HEAD_PAD), lambda bi, h, i: (bi, 0, h)),
                  pl.BlockSpec((1, l, V_HEAD), lambda bi, h, i: (bi, 0, h)),
                  pl.BlockSpec((1, c, HEAD_PAD), lambda bi, h, i: (bi, 0, h)),
                  pl.BlockSpec((1, c, V_HEAD), lambda bi, h, i: (bi, 0, h))],
        out_specs=pl.BlockSpec((1, tq, V_HEAD), lambda bi, h, i: (bi, i, h)),
        compiler_params=_params("arbitrary", "arbitrary", "arbitrary"),
        name="attention",
    )(q, k_lat, v_lat, k_ctx, v_ctx)


def _merge_kernel(gp_ref, ga_ref, pool_ref, hb_ref, ha_ref, attn_ref, pw_ref, ps_ref, wpu_ref, wao_ref, o_ref,
                  *, tm, seq):
    i = pl.program_id(1)
    hb = hb_ref[0]
    ha = ha_ref[0]
    hb = jnp.where(i == 0, jnp.zeros_like(hb), hb)
    ha = jnp.where(i == pl.num_programs(1) - 1, jnp.zeros_like(ha), ha)
    u = pool_ref[0]
    uext = jnp.concatenate([hb, u, ha], axis=0)
    rows = lax.broadcasted_iota(jnp.int32, (tm, tm + 2 * HALO), 0)
    cols = lax.broadcasted_iota(jnp.int32, (tm, tm + 2 * HALO), 1)
    off = cols - HALO - rows
    t = i * tm + lax.broadcasted_iota(jnp.int32, (tm, 1), 0)
    mixed = []
    for g, w in enumerate(POOL_WINDOWS):
        lo, hi = g * POOL_GROUP_DIM, (g + 1) * POOL_GROUP_DIM
        band = jnp.where((off >= -(w // 2)) & (off < w // 2), 1.0, 0.0).astype(BF16)
        wsum = jnp.dot(band, uext[:, lo:hi], preferred_element_type=F32)
        cnt = (jnp.minimum(t + w // 2, seq) - jnp.maximum(t - w // 2, 0)).astype(F32)
        pooled = wsum / cnt - u[:, lo:hi].astype(F32)
        mixed.append(jnp.dot(pooled.astype(BF16), pw_ref[g], preferred_element_type=F32))
    pool_out = jnp.concatenate(mixed, axis=-1) * ps_ref[...]
    pool_up = jnp.dot(pool_out.astype(BF16), wpu_ref[...], preferred_element_type=F32)
    attn_o = jnp.dot(attn_ref[0], wao_ref[...], preferred_element_type=F32)
    merged = _sigmoid(gp_ref[0].astype(F32)) * pool_up + _sigmoid(ga_ref[0].astype(F32)) * attn_o
    o_ref[0] = merged.astype(o_ref.dtype)


def _merge(p1, attn, pool_w, pool_scale, w_pool_up, w_attn_o, tm):
    b, l, _ = p1.shape
    hb = tm // HALO
    nhalo = l // HALO
    return pl.pallas_call(
        functools.partial(_merge_kernel, tm=tm, seq=l),
        out_shape=jax.ShapeDtypeStruct((b, l, D_MODEL), BF16),
        grid=(b, l // tm),
        in_specs=[pl.BlockSpec((1, tm, D_MODEL), lambda bi, i: (bi, i, COL_GATE_POOL // D_MODEL)),
                  pl.BlockSpec((1, tm, D_MODEL), lambda bi, i: (bi, i, COL_GATE_ATTN // D_MODEL)),
                  pl.BlockSpec((1, tm, POOL_DIM), lambda bi, i: (bi, i, COL_POOL // POOL_DIM)),
                  pl.BlockSpec((1, HALO, POOL_DIM),
                               lambda bi, i: (bi, jnp.maximum(i * hb - 1, 0), COL_POOL // POOL_DIM)),
                  pl.BlockSpec((1, HALO, POOL_DIM),
                               lambda bi, i: (bi, jnp.minimum((i + 1) * hb, nhalo - 1), COL_POOL // POOL_DIM)),
                  pl.BlockSpec((1, tm, D_MODEL), lambda bi, i: (bi, i, 0)),
                  _const_spec((POOL_GROUPS, POOL_GROUP_DIM, POOL_GROUP_DIM)),
                  _const_spec((1, POOL_DIM)),
                  _const_spec((POOL_DIM, D_MODEL)),
                  _const_spec((D_MODEL, D_MODEL))],
        out_specs=pl.BlockSpec((1, tm, D_MODEL), lambda bi, i: (bi, i, 0)),
        compiler_params=_params("arbitrary", "arbitrary"),
        name="merge",
    )(p1, p1, p1, p1, p1, attn, pool_w, pool_scale.reshape(1, POOL_DIM), w_pool_up, w_attn_o)


ROUTE_E1, ROUTE_E2, ROUTE_W1, ROUTE_W2 = 32, 33, 34, 35
GROUP_LANE0 = N_EXPERTS


def _route(logits):
    tm = logits.shape[0]
    lane = lax.broadcasted_iota(jnp.int32, (tm, LANES), 1)
    lanef = lane.astype(F32)
    neg = jnp.float32(-jnp.inf)
    big = jnp.float32(1e9)
    is_g = (lane >= GROUP_LANE0) & (lane < GROUP_LANE0 + N_GROUPS)
    lg = jnp.where(is_g, logits, neg)
    mg = jnp.max(lg, axis=-1, keepdims=True)
    pg_top = 1.0 / jnp.sum(jnp.exp(lg - mg), axis=-1, keepdims=True)
    g_idx = jnp.min(jnp.where(lg == mg, lanef, big), axis=-1, keepdims=True) - GROUP_LANE0
    e_lo = g_idx * EXPERTS_PER_GROUP
    in_grp = (lanef >= e_lo) & (lanef < e_lo + EXPERTS_PER_GROUP)
    ls = jnp.where(in_grp, logits, neg)
    v1 = jnp.max(ls, axis=-1, keepdims=True)
    i1 = jnp.min(jnp.where(ls == v1, lanef, big), axis=-1, keepdims=True)
    ls2 = jnp.where(lanef == i1, neg, ls)
    v2 = jnp.max(ls2, axis=-1, keepdims=True)
    i2 = jnp.min(jnp.where(ls2 == v2, lanef, big), axis=-1, keepdims=True)
    den = jnp.sum(jnp.exp(ls - v1), axis=-1, keepdims=True)
    pe1 = 1.0 / den
    pe2 = jnp.exp(v2 - v1) / den
    w1 = pg_top * pe1 / (pe1 + pe2)
    w2 = pg_top * pe2 / (pe1 + pe2)
    rec = jnp.where(lanef == i1, w1, 0.0) + jnp.where(lanef == i2, w2, 0.0)
    rec = jnp.where(lane == ROUTE_E1, i1, rec)
    rec = jnp.where(lane == ROUTE_E2, i2, rec)
    rec = jnp.where(lane == ROUTE_W1, w1, rec)
    rec = jnp.where(lane == ROUTE_W2, w2, rec)
    return rec


def _out_route_kernel(m_ref, x_ref, wo_ref, gpost_ref, g1_ref, gpre_ref, sh_ref, sc_ref, wrh_ref, wrl_ref, br_ref,
                      x1_ref, h2_ref, route_ref):
    y = jnp.dot(m_ref[0], wo_ref[...], preferred_element_type=F32)
    x1 = x_ref[0] + g1_ref[0] * _rms(y, gpost_ref[...])
    x1_ref[0] = x1
    h2 = _rms(x1, gpre_ref[...]) * (1.0 + sc_ref[0]) + sh_ref[0]
    hi = h2.astype(BF16)
    lo = (h2 - hi.astype(F32)).astype(BF16)
    h2_ref[0] = hi
    logits = (jnp.dot(hi, wrh_ref[...], preferred_element_type=F32)
              + jnp.dot(lo, wrh_ref[...], preferred_element_type=F32)
              + jnp.dot(hi, wrl_ref[...], preferred_element_type=F32)) + br_ref[...]
    route_ref[0] = _route(logits)


def _out_route(merged, x, w_out, g_post, g1, g_pre, sh2, sc2, wr_hi, wr_lo, br, tm):
    b, l, _ = x.shape
    tile = lambda w: pl.BlockSpec((1, tm, w), lambda bi, i: (bi, i, 0))
    per_b = pl.BlockSpec((1, 1, D_MODEL), lambda bi, i: (bi, 0, 0))
    return pl.pallas_call(
        _out_route_kernel,
        out_shape=(jax.ShapeDtypeStruct((b, l, D_MODEL), F32),
                   jax.ShapeDtypeStruct((b, l, D_MODEL), BF16),
                   jax.ShapeDtypeStruct((b, l, LANES), F32)),
        grid=(b, l // tm),
        in_specs=[tile(D_MODEL), tile(D_MODEL),
                  _const_spec((D_MODEL, D_MODEL)),
                  _const_spec((1, D_MODEL)),
                  per_b,
                  _const_spec((1, D_MODEL)),
                  per_b, per_b,
                  _const_spec((D_MODEL, LANES)),
                  _const_spec((D_MODEL, LANES)),
                  _const_spec((1, LANES))],
        out_specs=(tile(D_MODEL), tile(D_MODEL), tile(LANES)),
        compiler_params=_params("arbitrary", "arbitrary"),
        name="out_route",
    )(merged, x, w_out, g_post.reshape(1, D_MODEL), g1, g_pre.reshape(1, D_MODEL), sh2, sc2, wr_hi, wr_lo, br)


def _moe_kernel(h_ref, route_ref, wg_ref, wu_ref, wd_ref, y_ref):
    e = pl.program_id(1)

    @pl.when(e == 0)
    def _():
        y_ref[...] = jnp.zeros_like(y_ref)

    h = h_ref[...]
    gate = jnp.dot(h, wg_ref[0], preferred_element_type=F32)
    up = jnp.dot(h, wu_ref[0], preferred_element_type=F32)
    act = (gate * _sigmoid(gate)) * up
    route = route_ref[...]
    lane = lax.broadcasted_iota(jnp.int32, route.shape, 1)
    cw = jnp.sum(jnp.where(lane == e, route, 0.0), axis=-1, keepdims=True)
    y_ref[...] += cw * jnp.dot(act.astype(BF16), wd_ref[0], preferred_element_type=F32)


def _moe(h2, route, w_gate, w_up, w_down, tm):
    t = h2.shape[0]
    return pl.pallas_call(
        _moe_kernel,
        out_shape=jax.ShapeDtypeStruct((t, D_MODEL), F32),
        grid=(t // tm, N_EXPERTS),
        in_specs=[pl.BlockSpec((tm, D_MODEL), lambda i, e: (i, 0)),
                  pl.BlockSpec((tm, LANES), lambda i, e: (i, 0)),
                  pl.BlockSpec((1, D_MODEL, D_EXPERT), lambda i, e: (e, 0, 0)),
                  pl.BlockSpec((1, D_MODEL, D_EXPERT), lambda i, e: (e, 0, 0)),
                  pl.BlockSpec((1, D_EXPERT, D_MODEL), lambda i, e: (e, 0, 0))],
        out_specs=pl.BlockSpec((tm, D_MODEL), lambda i, e: (i, 0)),
        compiler_params=_params("arbitrary", "arbitrary"),
        name="moe",
    )(h2, route, w_gate, w_up, w_down)


def _final_kernel(x_ref, y_ref, g_ref, g2_ref, o_ref):
    o_ref[0] = x_ref[0] + g2_ref[0] * _rms(y_ref[0], g_ref[...])


def _final(x1, y2, g_post, g2, tm):
    b, l, _ = x1.shape
    tile = pl.BlockSpec((1, tm, D_MODEL), lambda bi, i: (bi, i, 0))
    return pl.pallas_call(
        _final_kernel,
        out_shape=jax.ShapeDtypeStruct((b, l, D_MODEL), F32),
        grid=(b, l // tm),
        in_specs=[tile, tile, _const_spec((1, D_MODEL)),
                  pl.BlockSpec((1, 1, D_MODEL), lambda bi, i: (bi, 0, 0))],
        out_specs=tile,
        compiler_params=_params("arbitrary", "arbitrary"),
        name="final",
    )(x1, y2, g_post.reshape(1, D_MODEL), g2)


def _rope_perm():
    j = np.arange(QK_ROPE)
    first_half = (j % (2 * ROPE_FREQS)) < ROPE_FREQS
    idx = np.where(first_half, j + ROPE_FREQS, j - ROPE_FREQS)
    sign = np.where(first_half, -1.0, 1.0).astype(np.float32)
    return idx, sign


def _rope_tables(seq):
    rows = seq // GRID_W
    row = jnp.repeat(jnp.arange(rows, dtype=F32), GRID_W)
    col = jnp.tile(jnp.arange(GRID_W, dtype=F32), rows)
    freqs = 1.0 / (ROPE_BASE ** (jnp.arange(ROPE_FREQS, dtype=F32) / ROPE_FREQS))
    ang_r = row[:, None] * freqs[None, :]
    ang_c = col[:, None] * freqs[None, :]
    ang = jnp.concatenate([ang_r, ang_r, ang_c, ang_c], axis=-1)
    cos, sin = jnp.cos(ang), jnp.sin(ang)
    tabk = jnp.concatenate([cos, sin], axis=-1)
    tabq = jnp.concatenate([jnp.full((seq, QK_NOPE), ATTN_SCALE, F32), tabk * ATTN_SCALE], axis=-1)
    return tabq, tabk


def _tile(n, pref):
    return pref if n % pref == 0 else n


def kernel(x, c, ctx, c_ctx, w_mod, b_mod, norm_pre_mix, norm_post_mix, norm_pre_ffn, norm_post_ffn, w_in, pool_w,
           pool_scale, w_pool_up, q_norm, w_uq, kv_norm, w_ukv, w_attn_o, w_out, router_group_w, router_group_b,
           router_expert_w, router_expert_b, w_gate, w_up, w_down):
    bsz, seq, _ = x.shape
    assert w_mod.shape[0] == 1, "single-layer block"
    idx, sign = _rope_perm()

    rows = -(-(bsz + 1) // 8) * 8
    cvec = jnp.concatenate([c, c_ctx[None, :], jnp.zeros((rows - bsz - 1, D_MODEL), F32)], axis=0)
    mod = _modulation(cvec, w_mod[0], b_mod[0])
    mod_x = mod[:bsz].reshape(bsz, N_MOD, 1, D_MODEL)
    sh1, sc1, g1, sh2, sc2, g2 = (mod_x[:, k] for k in range(N_MOD))
    mod_c = mod[bsz].reshape(N_MOD, 1, 1, D_MODEL)
    sh1c, sc1c = mod_c[0], mod_c[1]

    wi = w_in[0]
    s0, s1, s2, s3 = POOL_DIM, POOL_DIM + Q_LORA, POOL_DIM + Q_LORA + KV_LORA, POOL_DIM + Q_LORA + KV_LORA + QK_ROPE
    w_pool, w_qc, w_kvc, w_kr, w_gates = wi[:, :s0], wi[:, s0:s1], wi[:, s1:s2], wi[:, s2:s3], wi[:, s3:]
    w_kr_rot = w_kr[:, idx] * sign
    w1 = jnp.concatenate([w_gates, w_pool, w_qc, w_kvc, w_kr, w_kr_rot,
                          jnp.zeros((D_MODEL, IN_COLS_PAD - IN_COLS), F32)], axis=1).astype(BF16)
    w1c = jnp.concatenate([w_kvc, w_kr, w_kr], axis=1).astype(BF16)
    p1 = _in_proj(x, norm_pre_mix[0], sh1, sc1, w1, _tile(seq, 1024), IN_BLOCK_N)
    pc = _in_proj(ctx, norm_pre_mix[0], sh1c, sc1c, w1c, _tile(ctx.shape[1], 1024), w1c.shape[1])

    wq = w_uq[0].reshape(Q_LORA, MLA_HEADS, QK_HEAD)
    wq_rope = wq[:, :, QK_NOPE:]
    wq_ext = jnp.concatenate([wq, wq_rope[:, :, idx] * sign], axis=-1).reshape(Q_LORA, MLA_HEADS * HEAD_PAD)
    wkv = w_ukv[0].reshape(KV_LORA, MLA_HEADS, QK_NOPE + V_HEAD)
    wkv_re = jnp.concatenate([wkv[:, :, :QK_NOPE].reshape(KV_LORA, -1), wkv[:, :, QK_NOPE:].reshape(KV_LORA, -1)],
                             axis=1)
    wq_ext, wkv_re = wq_ext.astype(BF16), wkv_re.astype(BF16)
    tabq, tabk = _rope_tables(seq)
    q, k_lat, v_lat = _mla_latent(p1, q_norm[0], kv_norm[0], wq_ext, wkv_re, tabq, tabk, _tile(seq, 512))
    k_ctx, v_ctx = _mla_ctx(pc, kv_norm[0], wkv_re, _tile(ctx.shape[1], 256))

    attn = _attention(q, k_lat, v_lat, k_ctx, v_ctx, _tile(seq, 512))

    merged = _merge(p1, attn, pool_w[0].astype(BF16), pool_scale[0], w_pool_up[0].astype(BF16),
                    w_attn_o[0].astype(BF16), _tile(seq, 512))

    wr = jnp.concatenate([router_expert_w[0], router_group_w[0],
                          jnp.zeros((D_MODEL, LANES - N_EXPERTS - N_GROUPS), F32)], axis=1)
    wr_hi = wr.astype(BF16)
    wr_lo = (wr - wr_hi.astype(F32)).astype(BF16)
    br = jnp.concatenate([router_expert_b[0], router_group_b[0],
                          jnp.zeros((LANES - N_EXPERTS - N_GROUPS,), F32)]).reshape(1, LANES)
    x1, h2, route = _out_route(merged, x, w_out[0].astype(BF16), norm_post_mix[0], g1, norm_pre_ffn[0], sh2, sc2,
                               wr_hi, wr_lo, br, _tile(seq, 512))

    tokens = bsz * seq
    y2 = _moe(h2.reshape(tokens, D_MODEL), route.reshape(tokens, LANES), w_gate[0].astype(BF16),
              w_up[0].astype(BF16), w_down[0].astype(BF16), _tile(tokens, 1024))

    return _final(x1, y2.reshape(bsz, seq, D_MODEL), norm_post_ffn[0], g2, _tile(seq, 1024))
```

```python
import functools

import jax
import jax.numpy as jnp
import numpy as np
from jax import lax
from jax.experimental import pallas as pl
from jax.experimental.pallas import tpu as pltpu

F32 = jnp.float32
BF16 = jnp.bfloat16

D_MODEL = 2048
N_MOD = 6
EPS = 1e-6
GRID_W = 64
POOL_WINDOWS = (2, 4, 8, 16)
POOL_GROUPS = 4
POOL_GROUP_DIM = D_MODEL // 8
POOL_DIM = POOL_GROUPS * POOL_GROUP_DIM
MLA_HEADS = D_MODEL // 128
Q_LORA = D_MODEL // 4
KV_LORA = D_MODEL // 4
QK_NOPE = 128
QK_ROPE = 64
V_HEAD = 128
QK_HEAD = QK_NOPE + QK_ROPE
ROPE_FREQS = QK_ROPE // 4
ROPE_BASE = 10000.0
ATTN_SCALE = QK_HEAD ** -0.5
LOG2_E = 1.4426950408889634
N_GROUPS = 4
EXPERTS_PER_GROUP = 8
N_EXPERTS = N_GROUPS * EXPERTS_PER_GROUP
D_EXPERT = D_MODEL // 4

LANES = 128
HEAD_PAD = 2 * LANES
HALO = 16
ATTN_TQ = 512
ATTN_CHUNK = 512
V7X_VMEM_BYTES = 64 * 1024 * 1024
VMEM_LIMIT = 56 * 1024 * 1024

COL_GATE_POOL = 0
COL_GATE_ATTN = D_MODEL
COL_POOL = 2 * D_MODEL
COL_QC = COL_POOL + POOL_DIM
COL_KVC = COL_QC + Q_LORA
COL_KRR = COL_KVC + KV_LORA
IN_COLS = COL_KRR + LANES
IN_COLS_PAD = 6400
IN_BLOCK_N = 1280


def _params(*sem):
    return pltpu.CompilerParams(dimension_semantics=sem, vmem_limit_bytes=VMEM_LIMIT)


def _sigmoid(v):
    return 1.0 / (1.0 + jnp.exp(-v))


def _rms(v, g):
    v = v.astype(F32)
    return v * lax.rsqrt(jnp.mean(v * v, axis=-1, keepdims=True) + EPS) * g


def _const_spec(shape):
    nd = len(shape)
    return pl.BlockSpec(shape, lambda *_: (0,) * nd, pipeline_mode=pl.Buffered(1))


def _mod_kernel(c_ref, w_ref, b_ref, o_ref):
    a = c_ref[...]
    a = a * _sigmoid(a)
    o_ref[...] = jnp.dot(a.astype(BF16), w_ref[...].astype(BF16), preferred_element_type=F32) + b_ref[...]


def _modulation(cvec, w_mod, b_mod):
    rows = cvec.shape[0]
    n = w_mod.shape[1]
    bn = 1536
    return pl.pallas_call(
        _mod_kernel,
        out_shape=jax.ShapeDtypeStruct((rows, n), F32),
        grid=(n // bn,),
        in_specs=[pl.BlockSpec((rows, D_MODEL), lambda j: (0, 0)),
                  pl.BlockSpec((D_MODEL, bn), lambda j: (0, j)),
                  pl.BlockSpec((1, bn), lambda j: (0, j))],
        out_specs=pl.BlockSpec((rows, bn), lambda j: (0, j)),
        compiler_params=_params("arbitrary"),
        name="mod",
    )(cvec, w_mod, b_mod.reshape(1, n))


def _inproj_kernel(x_ref, g_ref, sh_ref, sc_ref, w_ref, o_ref, h_ref):
    @pl.when(pl.program_id(2) == 0)
    def _():
        h = _rms(x_ref[0], g_ref[...]) * (1.0 + sc_ref[0]) + sh_ref[0]
        h_ref[...] = h.astype(BF16)

    o_ref[0] = jnp.dot(h_ref[...], w_ref[...], preferred_element_type=F32).astype(o_ref.dtype)


def _in_proj(x, gain, shift, scale, w, tm, bn):
    b, l, _ = x.shape
    n = w.shape[1]
    per_batch = shift.shape[0] == b
    mod_map = (lambda bi, i, j: (bi, 0, 0)) if per_batch else (lambda bi, i, j: (0, 0, 0))
    return pl.pallas_call(
        _inproj_kernel,
        out_shape=jax.ShapeDtypeStruct((b, l, n), BF16),
        grid=(b, l // tm, n // bn),
        in_specs=[pl.BlockSpec((1, tm, D_MODEL), lambda bi, i, j: (bi, i, 0)),
                  pl.BlockSpec((1, D_MODEL), lambda bi, i, j: (0, 0)),
                  pl.BlockSpec((1, 1, D_MODEL), mod_map),
                  pl.BlockSpec((1, 1, D_MODEL), mod_map),
                  pl.BlockSpec((D_MODEL, bn), lambda bi, i, j: (0, j))],
        out_specs=pl.BlockSpec((1, tm, bn), lambda bi, i, j: (bi, i, j)),
        scratch_shapes=[pltpu.VMEM((tm, D_MODEL), BF16)],
        compiler_params=_params("arbitrary", "arbitrary", "arbitrary"),
        name="in_proj",
    )(x, gain.reshape(1, D_MODEL), shift, scale, w)


def _mla_latent_kernel(qc_ref, kvc_ref, krr_ref, qn_ref, kvn_ref, wq_ref, wkv_ref, tq_ref, tk_ref,
                       q_out, k_out, v_out):
    qn = _rms(qc_ref[0], qn_ref[...]).astype(BF16)
    q = jnp.dot(qn, wq_ref[...], preferred_element_type=F32)
    tq = tq_ref[...]
    for h in range(MLA_HEADS):
        q_out[0, :, h * HEAD_PAD:(h + 1) * HEAD_PAD] = (q[:, h * HEAD_PAD:(h + 1) * HEAD_PAD] * tq).astype(BF16)
    kvn = _rms(kvc_ref[0], kvn_ref[...]).astype(BF16)
    kv = jnp.dot(kvn, wkv_ref[...], preferred_element_type=F32)
    u = krr_ref[0].astype(F32) * tk_ref[...]
    krot = (u + pltpu.roll(u, QK_ROPE, axis=1)).astype(BF16)
    _store_kv(kv, krot, k_out, v_out)


def _store_kv(kv, k_rope, k_out, v_out):
    tm = kv.shape[0]
    lane = lax.broadcasted_iota(jnp.int32, (tm, LANES), 1)
    ones_col = jnp.where(lane == 0, 1.0, 0.0).astype(BF16)
    v0 = MLA_HEADS * QK_NOPE
    for h in range(MLA_HEADS):
        k_out[0, :, h * HEAD_PAD:h * HEAD_PAD + QK_NOPE] = kv[:, h * QK_NOPE:(h + 1) * QK_NOPE].astype(BF16)
        k_out[0, :, h * HEAD_PAD + QK_NOPE:(h + 1) * HEAD_PAD] = k_rope
        v_out[0, :, h * HEAD_PAD:h * HEAD_PAD + V_HEAD] = kv[:, v0 + h * V_HEAD:v0 + (h + 1) * V_HEAD].astype(BF16)
        v_out[0, :, h * HEAD_PAD + V_HEAD:(h + 1) * HEAD_PAD] = ones_col


def _mla_ctx_kernel(kvc_ref, krr_ref, kvn_ref, wkv_ref, k_out, v_out):
    kvn = _rms(kvc_ref[0], kvn_ref[...]).astype(BF16)
    kv = jnp.dot(kvn, wkv_ref[...], preferred_element_type=F32)
    _store_kv(kv, krr_ref[0], k_out, v_out)


def _mla_latent(p1, q_norm, kv_norm, wq, wkv, tabq, tabk, tm):
    b, l, _ = p1.shape
    hq = MLA_HEADS * HEAD_PAD
    hv = MLA_HEADS * HEAD_PAD
    return pl.pallas_call(
        _mla_latent_kernel,
        out_shape=(jax.ShapeDtypeStruct((b, l, hq), BF16),
                   jax.ShapeDtypeStruct((b, l, hq), BF16),
                   jax.ShapeDtypeStruct((b, l, hv), BF16)),
        grid=(b, l // tm),
        in_specs=[pl.BlockSpec((1, tm, Q_LORA), lambda bi, i: (bi, i, COL_QC // Q_LORA)),
                  pl.BlockSpec((1, tm, KV_LORA), lambda bi, i: (bi, i, COL_KVC // KV_LORA)),
                  pl.BlockSpec((1, tm, LANES), lambda bi, i: (bi, i, COL_KRR // LANES)),
                  _const_spec((1, Q_LORA)),
                  _const_spec((1, KV_LORA)),
                  _const_spec((Q_LORA, hq)),
                  _const_spec((KV_LORA, hq)),
                  pl.BlockSpec((tm, HEAD_PAD), lambda bi, i: (i, 0)),
                  pl.BlockSpec((tm, LANES), lambda bi, i: (i, 0))],
        out_specs=(pl.BlockSpec((1, tm, hq), lambda bi, i: (bi, i, 0)),
                   pl.BlockSpec((1, tm, hq), lambda bi, i: (bi, i, 0)),
                   pl.BlockSpec((1, tm, hv), lambda bi, i: (bi, i, 0))),
        compiler_params=_params("arbitrary", "arbitrary"),
        name="mla_latent",
    )(p1, p1, p1, q_norm.reshape(1, Q_LORA), kv_norm.reshape(1, KV_LORA), wq, wkv, tabq, tabk)


def _mla_ctx(pc, kv_norm, wkv, tm):
    b, c, _ = pc.shape
    hq = MLA_HEADS * HEAD_PAD
    hv = MLA_HEADS * HEAD_PAD
    return pl.pallas_call(
        _mla_ctx_kernel,
        out_shape=(jax.ShapeDtypeStruct((b, c, hq), BF16),
                   jax.ShapeDtypeStruct((b, c, hv), BF16)),
        grid=(b, c // tm),
        in_specs=[pl.BlockSpec((1, tm, KV_LORA), lambda bi, i: (bi, i, 0)),
                  pl.BlockSpec((1, tm, LANES), lambda bi, i: (bi, i, KV_LORA // LANES)),
                  _const_spec((1, KV_LORA)),
                  _const_spec((KV_LORA, hq))],
        out_specs=(pl.BlockSpec((1, tm, hq), lambda bi, i: (bi, i, 0)),
                   pl.BlockSpec((1, tm, hv), lambda bi, i: (bi, i, 0))),
        compiler_params=_params("arbitrary", "arbitrary"),
        name="mla_ctx",
    )(pc, pc, kv_norm.reshape(1, KV_LORA), wkv)


def _attn_kernel(q_ref, kl_ref, vl_ref, kc_ref, vc_ref, o_ref, *, chunk):
    q = q_ref[0]
    nt = (((1,), (1,)), ((), ()))
    seq = kl_ref.shape[1]
    segments = [(kl_ref, vl_ref, s0, chunk) for s0 in range(0, seq, chunk)] + [(kc_ref, vc_ref, 0, kc_ref.shape[1])]
    m = acc = None
    for k_ref, v_ref, s0, size in segments:
        s = lax.dot_general(q, k_ref[0, s0:s0 + size, :], nt, preferred_element_type=F32)
        m_seg = jnp.max(s, axis=-1, keepdims=True)
        m_new = m_seg if m is None else jnp.maximum(m, m_seg)
        p = jnp.exp2(s - m_new).astype(BF16)
        pv = jnp.dot(p, v_ref[0, s0:s0 + size, :], preferred_element_type=F32)
        acc = pv if m is None else acc * jnp.exp2(m - m_new) + pv
        m = m_new
    o_ref[0] = (acc[:, :V_HEAD] / acc[:, V_HEAD:V_HEAD + 1]).astype(o_ref.dtype)


def _attention(q, k_lat, v_lat, k_ctx, v_ctx, tq, chunk):
    b, l, _ = q.shape
    c = k_ctx.shape[1]
    return pl.pallas_call(
        functools.partial(_attn_kernel, chunk=chunk),
        out_shape=jax.ShapeDtypeStruct((b, l, MLA_HEADS * V_HEAD), BF16),
        grid=(b, MLA_HEADS, l // tq),
        in_specs=[pl.BlockSpec((1, tq, HEAD_PAD), lambda bi, h, i: (bi, i, h)),
                  pl.BlockSpec((1, l, HEAD_PAD), lambda bi, h, i: (bi, 0, h)),
                  pl.BlockSpec((1, l, HEAD_PAD), lambda bi, h, i: (bi, 0, h)),
                  pl.BlockSpec((1, c, HEAD_PAD), lambda bi, h, i: (bi, 0, h)),
                  pl.BlockSpec((1, c, HEAD_PAD), lambda bi, h, i: (bi, 0, h))],
        out_specs=pl.BlockSpec((1, tq, V_HEAD), lambda bi, h, i: (bi, i, h)),
        compiler_params=_params("arbitrary", "arbitrary", "arbitrary"),
        name="attention",
    )(q, k_lat, v_lat, k_ctx, v_ctx)


def _merge_kernel(gp_ref, ga_ref, pool_ref, hb_ref, ha_ref, attn_ref, pw_ref, ps_ref, wpu_ref, wao_ref, o_ref,
                  *, tm, seq):
    i = pl.program_id(1)
    hb = hb_ref[0]
    ha = ha_ref[0]
    hb = jnp.where(i == 0, jnp.zeros_like(hb), hb)
    ha = jnp.where(i == pl.num_programs(1) - 1, jnp.zeros_like(ha), ha)
    u = pool_ref[0]
    uext = jnp.concatenate([hb, u, ha], axis=0)
    rows = lax.broadcasted_iota(jnp.int32, (tm, tm + 2 * HALO), 0)
    cols = lax.broadcasted_iota(jnp.int32, (tm, tm + 2 * HALO), 1)
    off = cols - HALO - rows
    t = i * tm + lax.broadcasted_iota(jnp.int32, (tm, 1), 0)
    mixed = []
    for g, w in enumerate(POOL_WINDOWS):
        lo, hi = g * POOL_GROUP_DIM, (g + 1) * POOL_GROUP_DIM
        band = jnp.where((off >= -(w // 2)) & (off < w // 2), 1.0, 0.0).astype(BF16)
        wsum = jnp.dot(band, uext[:, lo:hi], preferred_element_type=F32)
        cnt = (jnp.minimum(t + w // 2, seq) - jnp.maximum(t - w // 2, 0)).astype(F32)
        pooled = wsum / cnt - u[:, lo:hi].astype(F32)
        mixed.append(jnp.dot(pooled.astype(BF16), pw_ref[g], preferred_element_type=F32))
    pool_out = jnp.concatenate(mixed, axis=-1) * ps_ref[...]
    pool_up = jnp.dot(pool_out.astype(BF16), wpu_ref[...], preferred_element_type=F32)
    attn_o = jnp.dot(attn_ref[0], wao_ref[...], preferred_element_type=F32)
    merged = _sigmoid(gp_ref[0].astype(F32)) * pool_up + _sigmoid(ga_ref[0].astype(F32)) * attn_o
    o_ref[0] = merged.astype(o_ref.dtype)


def _merge(p1, attn, pool_w, pool_scale, w_pool_up, w_attn_o, tm):
    b, l, _ = p1.shape
    hb = tm // HALO
    nhalo = l // HALO
    return pl.pallas_call(
        functools.partial(_merge_kernel, tm=tm, seq=l),
        out_shape=jax.ShapeDtypeStruct((b, l, D_MODEL), BF16),
        grid=(b, l // tm),
        in_specs=[pl.BlockSpec((1, tm, D_MODEL), lambda bi, i: (bi, i, COL_GATE_POOL // D_MODEL)),
                  pl.BlockSpec((1, tm, D_MODEL), lambda bi, i: (bi, i, COL_GATE_ATTN // D_MODEL)),
                  pl.BlockSpec((1, tm, POOL_DIM), lambda bi, i: (bi, i, COL_POOL // POOL_DIM)),
                  pl.BlockSpec((1, HALO, POOL_DIM),
                               lambda bi, i: (bi, jnp.maximum(i * hb - 1, 0), COL_POOL // POOL_DIM)),
                  pl.BlockSpec((1, HALO, POOL_DIM),
                               lambda bi, i: (bi, jnp.minimum((i + 1) * hb, nhalo - 1), COL_POOL // POOL_DIM)),
                  pl.BlockSpec((1, tm, D_MODEL), lambda bi, i: (bi, i, 0)),
                  _const_spec((POOL_GROUPS, POOL_GROUP_DIM, POOL_GROUP_DIM)),
                  _const_spec((1, POOL_DIM)),
                  _const_spec((POOL_DIM, D_MODEL)),
                  _const_spec((D_MODEL, D_MODEL))],
        out_specs=pl.BlockSpec((1, tm, D_MODEL), lambda bi, i: (bi, i, 0)),
        compiler_params=_params("arbitrary", "arbitrary"),
        name="merge",
    )(p1, p1, p1, p1, p1, attn, pool_w, pool_scale.reshape(1, POOL_DIM), w_pool_up, w_attn_o)


ROUTE_E1, ROUTE_E2, ROUTE_W1, ROUTE_W2, ROUTE_R1, ROUTE_R2 = 0, 1, 2, 3, 4, 5
GROUP_LANE0 = N_EXPERTS
PACK = D_MODEL // 2
MOE_TILE = 256
MOE_TILES_MAX_EXTRA = N_EXPERTS


def _route(logits, carry):
    tm = logits.shape[0]
    lane = lax.broadcasted_iota(jnp.int32, (tm, LANES), 1)
    lanef = lane.astype(F32)
    neg = jnp.float32(-jnp.inf)
    big = jnp.float32(1e9)
    is_g = (lane >= GROUP_LANE0) & (lane < GROUP_LANE0 + N_GROUPS)
    lg = jnp.where(is_g, logits, neg)
    mg = jnp.max(lg, axis=-1, keepdims=True)
    pg_top = 1.0 / jnp.sum(jnp.exp(lg - mg), axis=-1, keepdims=True)
    g_idx = jnp.min(jnp.where(lg == mg, lanef, big), axis=-1, keepdims=True) - GROUP_LANE0
    e_lo = g_idx * EXPERTS_PER_GROUP
    in_grp = (lanef >= e_lo) & (lanef < e_lo + EXPERTS_PER_GROUP)
    ls = jnp.where(in_grp, logits, neg)
    v1 = jnp.max(ls, axis=-1, keepdims=True)
    i1 = jnp.min(jnp.where(ls == v1, lanef, big), axis=-1, keepdims=True)
    ls2 = jnp.where(lanef == i1, neg, ls)
    v2 = jnp.max(ls2, axis=-1, keepdims=True)
    i2 = jnp.min(jnp.where(ls2 == v2, lanef, big), axis=-1, keepdims=True)
    den = jnp.sum(jnp.exp(ls - v1), axis=-1, keepdims=True)
    pe1 = 1.0 / den
    pe2 = jnp.exp(v2 - v1) / den
    w1 = pg_top * pe1 / (pe1 + pe2)
    w2 = pg_top * pe2 / (pe1 + pe2)
    sel = jnp.where((lanef == i1) | (lanef == i2), 1.0, 0.0)
    r = lax.broadcasted_iota(jnp.int32, (tm, tm), 0)
    cc = lax.broadcasted_iota(jnp.int32, (tm, tm), 1)
    before = jnp.where(cc < r, 1.0, 0.0).astype(BF16)
    seen = jnp.dot(before, sel.astype(BF16), preferred_element_type=F32) + carry
    r1 = jnp.sum(jnp.where(lanef == i1, seen, 0.0), axis=-1, keepdims=True)
    r2 = jnp.sum(jnp.where(lanef == i2, seen, 0.0), axis=-1, keepdims=True)
    rec = jnp.zeros((tm, LANES), F32)
    for pos, val in ((ROUTE_E1, i1), (ROUTE_E2, i2), (ROUTE_W1, w1), (ROUTE_W2, w2), (ROUTE_R1, r1), (ROUTE_R2, r2)):
        rec = jnp.where(lane == pos, val, rec)
    return rec, carry + jnp.sum(sel, axis=0, keepdims=True)


def _pack_bf16_pairs(h):
    bits = lax.bitcast_convert_type(h.astype(F32), jnp.uint32)
    return (bits[:, :PACK] >> 16) | bits[:, PACK:]


def _unpack_bf16_pairs(words):
    lo = lax.bitcast_convert_type(words << 16, F32)
    hi = lax.bitcast_convert_type(words & jnp.uint32(0xFFFF0000), F32)
    return jnp.concatenate([lo, hi], axis=-1).astype(BF16)


def _out_route_kernel(m_ref, x_ref, wo_ref, gpost_ref, g1_ref, gpre_ref, sh_ref, sc_ref, wrh_ref, wrl_ref, br_ref,
                      x1_ref, hw_ref, route_ref, cnt_ref, carry_ref):
    @pl.when((pl.program_id(0) == 0) & (pl.program_id(1) == 0))
    def _():
        carry_ref[...] = jnp.zeros_like(carry_ref)

    y = jnp.dot(m_ref[0], wo_ref[...], preferred_element_type=F32)
    x1 = x_ref[0] + g1_ref[0] * _rms(y, gpost_ref[...])
    x1_ref[0] = x1
    h2 = _rms(x1, gpre_ref[...]) * (1.0 + sc_ref[0]) + sh_ref[0]
    hi = h2.astype(BF16)
    lo = (h2 - hi.astype(F32)).astype(BF16)
    hw_ref[0] = _pack_bf16_pairs(hi)
    logits = (jnp.dot(hi, wrh_ref[...], preferred_element_type=F32)
              + jnp.dot(lo, wrh_ref[...], preferred_element_type=F32)
              + jnp.dot(hi, wrl_ref[...], preferred_element_type=F32)) + br_ref[...]
    rec, carry = _route(logits, carry_ref[...])
    route_ref[0] = rec
    carry_ref[...] = carry
    cnt_ref[...] = carry


def _out_route(merged, x, w_out, g_post, g1, g_pre, sh2, sc2, wr_hi, wr_lo, br, tm):
    b, l, _ = x.shape
    tile = lambda w: pl.BlockSpec((1, tm, w), lambda bi, i: (bi, i, 0))
    per_b = pl.BlockSpec((1, 1, D_MODEL), lambda bi, i: (bi, 0, 0))
    return pl.pallas_call(
        _out_route_kernel,
        out_shape=(jax.ShapeDtypeStruct((b, l, D_MODEL), F32),
                   jax.ShapeDtypeStruct((b, l, PACK), jnp.uint32),
                   jax.ShapeDtypeStruct((b, l, LANES), F32),
                   jax.ShapeDtypeStruct((1, LANES), F32)),
        grid=(b, l // tm),
        in_specs=[tile(D_MODEL), tile(D_MODEL),
                  _const_spec((D_MODEL, D_MODEL)),
                  _const_spec((1, D_MODEL)),
                  per_b,
                  _const_spec((1, D_MODEL)),
                  per_b, per_b,
                  _const_spec((D_MODEL, LANES)),
                  _const_spec((D_MODEL, LANES)),
                  _const_spec((1, LANES))],
        out_specs=(tile(D_MODEL), tile(PACK), tile(LANES), pl.BlockSpec((1, LANES), lambda bi, i: (0, 0))),
        scratch_shapes=[pltpu.VMEM((1, LANES), F32)],
        compiler_params=_params("arbitrary", "arbitrary"),
        name="out_route",
    )(merged, x, w_out, g_post.reshape(1, D_MODEL), g1, g_pre.reshape(1, D_MODEL), sh2, sc2, wr_hi, wr_lo, br)


def _row_copy(src_ref, src_row, dst_ref, dst_row, sem):
    return pltpu.make_async_copy(src_ref.at[pl.ds(src_row, 1)], dst_ref.at[pl.ds(dst_row, 1)], sem)


def _dispatch_kernel(pad_start_ref, pad_len_ref, n_tiles_ref, dest_ref, h_ref, xs_ref, zero_ref, sem, *, tm, n_max):
    i = pl.program_id(0)

    def tile_copy(j):
        return pltpu.make_async_copy(zero_ref, xs_ref.at[pl.ds(j * MOE_TILE, MOE_TILE)], sem)

    def start_token(r, carry):
        for k in range(2):
            _row_copy(h_ref, i * tm + r, xs_ref, dest_ref[0, 0, 2 * r + k], sem).start()
        return carry

    def wait_token(r, carry):
        for k in range(2):
            _row_copy(h_ref, 0, xs_ref, 0, sem).wait()
        return carry

    lax.fori_loop(0, tm, start_token, 0)

    @pl.when(i == 0)
    def _():
        zero_ref[...] = jnp.zeros_like(zero_ref)

        def start_expert(e, carry):
            def start_row(j, c):
                _row_copy(zero_ref, 0, xs_ref, pad_start_ref[e] + j, sem).start()
                return c
            return lax.fori_loop(0, pad_len_ref[e], start_row, carry)

        def wait_expert(e, carry):
            def wait_row(j, c):
                _row_copy(zero_ref, 0, xs_ref, 0, sem).wait()
                return c
            return lax.fori_loop(0, pad_len_ref[e], wait_row, carry)

        lax.fori_loop(0, N_EXPERTS, start_expert, 0)
        lax.fori_loop(n_tiles_ref[0], n_max, lambda j, c: (tile_copy(j).start(), c)[1], 0)
        lax.fori_loop(0, N_EXPERTS, wait_expert, 0)
        lax.fori_loop(n_tiles_ref[0], n_max, lambda j, c: (tile_copy(j).wait(), c)[1], 0)

    lax.fori_loop(0, tm, wait_token, 0)


def _dispatch(hw, dest, pad_start, pad_len, n_tiles, n_max, tm):
    t = hw.shape[0]
    grid_spec = pltpu.PrefetchScalarGridSpec(
        num_scalar_prefetch=3,
        grid=(t // tm,),
        in_specs=[pl.BlockSpec((1, 1, 2 * tm), lambda i, *_: (i, 0, 0), memory_space=pltpu.SMEM),
                  pl.BlockSpec(memory_space=pl.ANY)],
        out_specs=pl.BlockSpec(memory_space=pl.ANY),
        scratch_shapes=[pltpu.VMEM((MOE_TILE, PACK), jnp.uint32), pltpu.SemaphoreType.DMA(())],
    )
    return pl.pallas_call(
        functools.partial(_dispatch_kernel, tm=tm, n_max=n_max),
        out_shape=jax.ShapeDtypeStruct((n_max * MOE_TILE, PACK), jnp.uint32),
        grid_spec=grid_spec,
        compiler_params=_params("arbitrary"),
        name="dispatch",
    )(pad_start, pad_len, n_tiles, dest.reshape(t // tm, 1, 2 * tm), hw)


def _moe_kernel(tile_expert_ref, tile_first_ref, n_tiles_ref, x_ref, wg_ref, wu_ref, wd_ref, y_ref,
                wg_bf, wu_bf, wd_bf):
    j = pl.program_id(0)

    @pl.when(j < n_tiles_ref[0])
    def _():
        @pl.when(tile_first_ref[j] == 1)
        def _():
            wg_bf[...] = wg_ref[0].astype(BF16)
            wu_bf[...] = wu_ref[0].astype(BF16)
            wd_bf[...] = wd_ref[0].astype(BF16)

        h = _unpack_bf16_pairs(x_ref[...])
        gate = jnp.dot(h, wg_bf[...], preferred_element_type=F32)
        up = jnp.dot(h, wu_bf[...], preferred_element_type=F32)
        act = (gate * _sigmoid(gate)) * up
        y_ref[...] = jnp.dot(act.astype(BF16), wd_bf[...], preferred_element_type=F32)

    @pl.when(j >= n_tiles_ref[0])
    def _():
        y_ref[...] = jnp.zeros_like(y_ref)


def _moe(xs, tile_expert, tile_first, n_tiles, w_gate, w_up, w_down):
    n_max = xs.shape[0] // MOE_TILE
    row_map = lambda j, te, tf, nt: (jnp.minimum(j, nt[0] - 1), 0)
    out_map = lambda j, te, tf, nt: (j, 0)
    w_map = lambda j, te, tf, nt: (te[j], 0, 0)
    grid_spec = pltpu.PrefetchScalarGridSpec(
        num_scalar_prefetch=3,
        grid=(n_max,),
        in_specs=[pl.BlockSpec((MOE_TILE, PACK), row_map),
                  pl.BlockSpec((1, D_MODEL, D_EXPERT), w_map),
                  pl.BlockSpec((1, D_MODEL, D_EXPERT), w_map),
                  pl.BlockSpec((1, D_EXPERT, D_MODEL), w_map)],
        out_specs=pl.BlockSpec((MOE_TILE, D_MODEL), out_map),
        scratch_shapes=[pltpu.VMEM((D_MODEL, D_EXPERT), BF16), pltpu.VMEM((D_MODEL, D_EXPERT), BF16),
                        pltpu.VMEM((D_EXPERT, D_MODEL), BF16)],
    )
    return pl.pallas_call(
        _moe_kernel,
        out_shape=jax.ShapeDtypeStruct((xs.shape[0], D_MODEL), F32),
        grid_spec=grid_spec,
        compiler_params=_params("arbitrary"),
        name="moe",
    )(tile_expert, tile_first, n_tiles, xs, w_gate, w_up, w_down)


def _final_kernel(dest_ref, x_ref, route_ref, ys_ref, g_ref, g2_ref, o_ref, rows_ref, sem, *, tm):
    def start_token(r, carry):
        for k in range(2):
            _row_copy(ys_ref, dest_ref[0, 0, 2 * r + k], rows_ref.at[k], r, sem).start()
        return carry

    def wait_token(r, carry):
        for k in range(2):
            _row_copy(ys_ref, 0, rows_ref.at[k], 0, sem).wait()
        return carry

    lax.fori_loop(0, tm, start_token, 0)
    lax.fori_loop(0, tm, wait_token, 0)
    route = route_ref[0]
    w1 = route[:, ROUTE_W1:ROUTE_W1 + 1]
    w2 = route[:, ROUTE_W2:ROUTE_W2 + 1]
    y = w1 * rows_ref[0] + w2 * rows_ref[1]
    o_ref[0] = x_ref[0] + g2_ref[0] * _rms(y, g_ref[...])


def _final(x1, route, ys, dest, g_post, g2, tm):
    b, l, _ = x1.shape
    nt = l // tm
    grid_spec = pltpu.PrefetchScalarGridSpec(
        num_scalar_prefetch=0,
        grid=(b, nt),
        in_specs=[pl.BlockSpec((1, 1, 2 * tm), lambda bi, i: (bi * nt + i, 0, 0), memory_space=pltpu.SMEM),
                  pl.BlockSpec((1, tm, D_MODEL), lambda bi, i: (bi, i, 0)),
                  pl.BlockSpec((1, tm, LANES), lambda bi, i: (bi, i, 0)),
                  pl.BlockSpec(memory_space=pl.ANY),
                  _const_spec((1, D_MODEL)),
                  pl.BlockSpec((1, 1, D_MODEL), lambda bi, i: (bi, 0, 0))],
        out_specs=pl.BlockSpec((1, tm, D_MODEL), lambda bi, i: (bi, i, 0)),
        scratch_shapes=[pltpu.VMEM((2, tm, D_MODEL), F32), pltpu.SemaphoreType.DMA(())],
    )
    return pl.pallas_call(
        functools.partial(_final_kernel, tm=tm),
        out_shape=jax.ShapeDtypeStruct((b, l, D_MODEL), F32),
        grid_spec=grid_spec,
        compiler_params=_params("arbitrary", "arbitrary"),
        name="final",
    )(dest.reshape(b * nt, 1, 2 * tm), x1, route, ys, g_post.reshape(1, D_MODEL), g2)


def _rope_perm():
    j = np.arange(QK_ROPE)
    first_half = (j % (2 * ROPE_FREQS)) < ROPE_FREQS
    idx = np.where(first_half, j + ROPE_FREQS, j - ROPE_FREQS)
    sign = np.where(first_half, -1.0, 1.0).astype(np.float32)
    return idx, sign


def _rope_tables(seq):
    rows = seq // GRID_W
    row = jnp.repeat(jnp.arange(rows, dtype=F32), GRID_W)
    col = jnp.tile(jnp.arange(GRID_W, dtype=F32), rows)
    freqs = 1.0 / (ROPE_BASE ** (jnp.arange(ROPE_FREQS, dtype=F32) / ROPE_FREQS))
    ang_r = row[:, None] * freqs[None, :]
    ang_c = col[:, None] * freqs[None, :]
    ang = jnp.concatenate([ang_r, ang_r, ang_c, ang_c], axis=-1)
    cos, sin = jnp.cos(ang), jnp.sin(ang)
    tabk = jnp.concatenate([cos, sin], axis=-1)
    qs = ATTN_SCALE * LOG2_E
    tabq = jnp.concatenate([jnp.full((seq, QK_NOPE), qs, F32), tabk * qs], axis=-1)
    return tabq, tabk


def _tile(n, pref):
    return pref if n % pref == 0 else n


def kernel(x, c, ctx, c_ctx, w_mod, b_mod, norm_pre_mix, norm_post_mix, norm_pre_ffn, norm_post_ffn, w_in, pool_w,
           pool_scale, w_pool_up, q_norm, w_uq, kv_norm, w_ukv, w_attn_o, w_out, router_group_w, router_group_b,
           router_expert_w, router_expert_b, w_gate, w_up, w_down):
    bsz, seq, _ = x.shape
    assert w_mod.shape[0] == 1, "single-layer block"
    idx, sign = _rope_perm()

    rows = -(-(bsz + 1) // 8) * 8
    cvec = jnp.concatenate([c, c_ctx[None, :], jnp.zeros((rows - bsz - 1, D_MODEL), F32)], axis=0)
    mod = _modulation(cvec, w_mod[0], b_mod[0])
    mod_x = mod[:bsz].reshape(bsz, N_MOD, 1, D_MODEL)
    sh1, sc1, g1, sh2, sc2, g2 = (mod_x[:, k] for k in range(N_MOD))
    mod_c = mod[bsz].reshape(N_MOD, 1, 1, D_MODEL)
    sh1c, sc1c = mod_c[0], mod_c[1]

    wi = w_in[0]
    s0, s1, s2, s3 = POOL_DIM, POOL_DIM + Q_LORA, POOL_DIM + Q_LORA + KV_LORA, POOL_DIM + Q_LORA + KV_LORA + QK_ROPE
    w_pool, w_qc, w_kvc, w_kr, w_gates = wi[:, :s0], wi[:, s0:s1], wi[:, s1:s2], wi[:, s2:s3], wi[:, s3:]
    w_kr_rot = w_kr[:, idx] * sign
    w1 = jnp.concatenate([w_gates, w_pool, w_qc, w_kvc, w_kr, w_kr_rot,
                          jnp.zeros((D_MODEL, IN_COLS_PAD - IN_COLS), F32)], axis=1).astype(BF16)
    w1c = jnp.concatenate([w_kvc, w_kr, w_kr], axis=1).astype(BF16)
    p1 = _in_proj(x, norm_pre_mix[0], sh1, sc1, w1, _tile(seq, 1024), IN_BLOCK_N)
    pc = _in_proj(ctx, norm_pre_mix[0], sh1c, sc1c, w1c, _tile(ctx.shape[1], 1024), w1c.shape[1])

    wq = w_uq[0].reshape(Q_LORA, MLA_HEADS, QK_HEAD)
    wq_rope = wq[:, :, QK_NOPE:]
    wq_ext = jnp.concatenate([wq, wq_rope[:, :, idx] * sign], axis=-1).reshape(Q_LORA, MLA_HEADS * HEAD_PAD)
    wkv = w_ukv[0].reshape(KV_LORA, MLA_HEADS, QK_NOPE + V_HEAD)
    wkv_re = jnp.concatenate([wkv[:, :, :QK_NOPE].reshape(KV_LORA, -1), wkv[:, :, QK_NOPE:].reshape(KV_LORA, -1)],
                             axis=1)
    wq_ext, wkv_re = wq_ext.astype(BF16), wkv_re.astype(BF16)
    tabq, tabk = _rope_tables(seq)
    q, k_lat, v_lat = _mla_latent(p1, q_norm[0], kv_norm[0], wq_ext, wkv_re, tabq, tabk, _tile(seq, 512))
    k_ctx, v_ctx = _mla_ctx(pc, kv_norm[0], wkv_re, _tile(ctx.shape[1], 256))

    attn = _attention(q, k_lat, v_lat, k_ctx, v_ctx, _tile(seq, ATTN_TQ), _tile(seq, ATTN_CHUNK))

    merged = _merge(p1, attn, pool_w[0].astype(BF16), pool_scale[0], w_pool_up[0].astype(BF16),
                    w_attn_o[0].astype(BF16), _tile(seq, 512))

    wr = jnp.concatenate([router_expert_w[0], router_group_w[0],
                          jnp.zeros((D_MODEL, LANES - N_EXPERTS - N_GROUPS), F32)], axis=1)
    wr_hi = wr.astype(BF16)
    wr_lo = (wr - wr_hi.astype(F32)).astype(BF16)
    br = jnp.concatenate([router_expert_b[0], router_group_b[0],
                          jnp.zeros((LANES - N_EXPERTS - N_GROUPS,), F32)]).reshape(1, LANES)
    x1, hw, route, counts = _out_route(merged, x, w_out[0].astype(BF16), norm_post_mix[0], g1, norm_pre_ffn[0],
                                       sh2, sc2, wr_hi, wr_lo, br, _tile(seq, 512))

    tokens = bsz * seq
    n_tiles_max = 2 * tokens // MOE_TILE + MOE_TILES_MAX_EXTRA
    cnt = counts[0, :N_EXPERTS].astype(jnp.int32)
    tiles_per = (cnt + MOE_TILE - 1) // MOE_TILE
    tile_end = jnp.cumsum(tiles_per)
    seg_start = (tile_end - tiles_per) * MOE_TILE
    n_tiles = tile_end[-1:]
    tile_id = jnp.minimum(jnp.arange(n_tiles_max, dtype=jnp.int32), n_tiles[0] - 1)
    tile_expert = jnp.sum((tile_id[:, None] >= tile_end[None, :]).astype(jnp.int32), axis=1)
    tile_first = jnp.concatenate([jnp.ones((1,), jnp.int32),
                                  (tile_expert[1:] != tile_expert[:-1]).astype(jnp.int32)])
    rt = route.reshape(tokens, LANES)
    experts = jnp.arange(N_EXPERTS, dtype=jnp.int32)[None, :]

    def dest_of(e_lane, r_lane):
        e = rt[:, e_lane].astype(jnp.int32)
        return rt[:, r_lane].astype(jnp.int32) + jnp.sum(jnp.where(e[:, None] == experts, seg_start[None, :], 0), axis=1)

    dest = jnp.stack([dest_of(ROUTE_E1, ROUTE_R1), dest_of(ROUTE_E2, ROUTE_R2)], axis=1)

    xs = _dispatch(hw.reshape(tokens, PACK), dest, seg_start + cnt, tiles_per * MOE_TILE - cnt,
                   n_tiles, n_tiles_max, _tile(tokens, 1024))
    ys = _moe(xs, tile_expert, tile_first, n_tiles, w_gate[0], w_up[0], w_down[0])
    return _final(x1, route, ys, dest, norm_post_ffn[0], g2, _tile(seq, 256))
```

```python
import functools

import jax
import jax.numpy as jnp
import numpy as np
from jax import lax
from jax.experimental import pallas as pl
from jax.experimental.pallas import tpu as pltpu

F32 = jnp.float32
BF16 = jnp.bfloat16

D_MODEL = 2048
N_MOD = 6
EPS = 1e-6
GRID_W = 64
POOL_WINDOWS = (2, 4, 8, 16)
POOL_GROUPS = 4
POOL_GROUP_DIM = D_MODEL // 8
POOL_DIM = POOL_GROUPS * POOL_GROUP_DIM
MLA_HEADS = D_MODEL // 128
Q_LORA = D_MODEL // 4
KV_LORA = D_MODEL // 4
QK_NOPE = 128
QK_ROPE = 64
V_HEAD = 128
QK_HEAD = QK_NOPE + QK_ROPE
ROPE_FREQS = QK_ROPE // 4
ROPE_BASE = 10000.0
ATTN_SCALE = QK_HEAD ** -0.5
LOG2_E = 1.4426950408889634
N_GROUPS = 4
EXPERTS_PER_GROUP = 8
N_EXPERTS = N_GROUPS * EXPERTS_PER_GROUP
D_EXPERT = D_MODEL // 4

LANES = 128
HEAD_PAD = 2 * LANES
HALO = 16
ATTN_TQ = 512
ATTN_CHUNK = 512
ATTN_HEADS = 2
V7X_VMEM_BYTES = 64 * 1024 * 1024
VMEM_LIMIT = 56 * 1024 * 1024

COL_GATE_POOL = 0
COL_GATE_ATTN = D_MODEL
COL_POOL = 2 * D_MODEL
COL_QC = COL_POOL + POOL_DIM
COL_KVC = COL_QC + Q_LORA
COL_KRR = COL_KVC + KV_LORA
IN_COLS = COL_KRR + LANES
IN_COLS_PAD = 6400
IN_BLOCK_N = 1280


def _params(*sem):
    return pltpu.CompilerParams(dimension_semantics=sem, vmem_limit_bytes=VMEM_LIMIT)


def _sigmoid(v):
    return 1.0 / (1.0 + jnp.exp(-v))


def _rms(v, g):
    v = v.astype(F32)
    return v * lax.rsqrt(jnp.mean(v * v, axis=-1, keepdims=True) + EPS) * g


def _const_spec(shape):
    nd = len(shape)
    return pl.BlockSpec(shape, lambda *_: (0,) * nd, pipeline_mode=pl.Buffered(1))


def _mod_kernel(c_ref, w_ref, b_ref, o_ref):
    a = c_ref[...]
    a = a * _sigmoid(a)
    o_ref[...] = jnp.dot(a.astype(BF16), w_ref[...].astype(BF16), preferred_element_type=F32) + b_ref[...]


def _modulation(cvec, w_mod, b_mod):
    rows = cvec.shape[0]
    n = w_mod.shape[1]
    bn = 1536
    return pl.pallas_call(
        _mod_kernel,
        out_shape=jax.ShapeDtypeStruct((rows, n), F32),
        grid=(n // bn,),
        in_specs=[pl.BlockSpec((rows, D_MODEL), lambda j: (0, 0)),
                  pl.BlockSpec((D_MODEL, bn), lambda j: (0, j)),
                  pl.BlockSpec((1, bn), lambda j: (0, j))],
        out_specs=pl.BlockSpec((rows, bn), lambda j: (0, j)),
        compiler_params=_params("arbitrary"),
        name="mod",
    )(cvec, w_mod, b_mod.reshape(1, n))


def _inproj_kernel(x_ref, g_ref, sh_ref, sc_ref, w_ref, o_ref, h_ref):
    @pl.when(pl.program_id(2) == 0)
    def _():
        h = _rms(x_ref[0], g_ref[...]) * (1.0 + sc_ref[0]) + sh_ref[0]
        h_ref[...] = h.astype(BF16)

    o_ref[0] = jnp.dot(h_ref[...], w_ref[...], preferred_element_type=F32).astype(o_ref.dtype)


def _in_proj(x, gain, shift, scale, w, tm, bn):
    b, l, _ = x.shape
    n = w.shape[1]
    per_batch = shift.shape[0] == b
    mod_map = (lambda bi, i, j: (bi, 0, 0)) if per_batch else (lambda bi, i, j: (0, 0, 0))
    return pl.pallas_call(
        _inproj_kernel,
        out_shape=jax.ShapeDtypeStruct((b, l, n), BF16),
        grid=(b, l // tm, n // bn),
        in_specs=[pl.BlockSpec((1, tm, D_MODEL), lambda bi, i, j: (bi, i, 0)),
                  pl.BlockSpec((1, D_MODEL), lambda bi, i, j: (0, 0)),
                  pl.BlockSpec((1, 1, D_MODEL), mod_map),
                  pl.BlockSpec((1, 1, D_MODEL), mod_map),
                  pl.BlockSpec((D_MODEL, bn), lambda bi, i, j: (0, j))],
        out_specs=pl.BlockSpec((1, tm, bn), lambda bi, i, j: (bi, i, j)),
        scratch_shapes=[pltpu.VMEM((tm, D_MODEL), BF16)],
        compiler_params=_params("arbitrary", "arbitrary", "arbitrary"),
        name="in_proj",
    )(x, gain.reshape(1, D_MODEL), shift, scale, w)


def _mla_latent_kernel(qc_ref, kvc_ref, krr_ref, qn_ref, kvn_ref, wq_ref, wkv_ref, tq_ref, tk_ref,
                       q_out, k_out, v_out):
    qn = _rms(qc_ref[0], qn_ref[...]).astype(BF16)
    q = jnp.dot(qn, wq_ref[...], preferred_element_type=F32)
    tq = tq_ref[...]
    for h in range(MLA_HEADS):
        q_out[0, :, h * HEAD_PAD:(h + 1) * HEAD_PAD] = (q[:, h * HEAD_PAD:(h + 1) * HEAD_PAD] * tq).astype(BF16)
    kvn = _rms(kvc_ref[0], kvn_ref[...]).astype(BF16)
    kv = jnp.dot(kvn, wkv_ref[...], preferred_element_type=F32)
    u = krr_ref[0].astype(F32) * tk_ref[...]
    krot = (u + pltpu.roll(u, QK_ROPE, axis=1)).astype(BF16)
    _store_kv(kv, krot, k_out, v_out)


def _store_kv(kv, k_rope, k_out, v_out):
    tm = kv.shape[0]
    lane = lax.broadcasted_iota(jnp.int32, (tm, LANES), 1)
    ones_col = jnp.where(lane == 0, 1.0, 0.0).astype(BF16)
    v0 = MLA_HEADS * QK_NOPE
    for h in range(MLA_HEADS):
        k_out[0, :, h * HEAD_PAD:h * HEAD_PAD + QK_NOPE] = kv[:, h * QK_NOPE:(h + 1) * QK_NOPE].astype(BF16)
        k_out[0, :, h * HEAD_PAD + QK_NOPE:(h + 1) * HEAD_PAD] = k_rope
        v_out[0, :, h * HEAD_PAD:h * HEAD_PAD + V_HEAD] = kv[:, v0 + h * V_HEAD:v0 + (h + 1) * V_HEAD].astype(BF16)
        v_out[0, :, h * HEAD_PAD + V_HEAD:(h + 1) * HEAD_PAD] = ones_col


def _mla_ctx_kernel(kvc_ref, krr_ref, kvn_ref, wkv_ref, k_out, v_out):
    kvn = _rms(kvc_ref[0], kvn_ref[...]).astype(BF16)
    kv = jnp.dot(kvn, wkv_ref[...], preferred_element_type=F32)
    _store_kv(kv, krr_ref[0], k_out, v_out)


def _mla_latent(p1, q_norm, kv_norm, wq, wkv, tabq, tabk, tm):
    b, l, _ = p1.shape
    hq = MLA_HEADS * HEAD_PAD
    hv = MLA_HEADS * HEAD_PAD
    return pl.pallas_call(
        _mla_latent_kernel,
        out_shape=(jax.ShapeDtypeStruct((b, l, hq), BF16),
                   jax.ShapeDtypeStruct((b, l, hq), BF16),
                   jax.ShapeDtypeStruct((b, l, hv), BF16)),
        grid=(b, l // tm),
        in_specs=[pl.BlockSpec((1, tm, Q_LORA), lambda bi, i: (bi, i, COL_QC // Q_LORA)),
                  pl.BlockSpec((1, tm, KV_LORA), lambda bi, i: (bi, i, COL_KVC // KV_LORA)),
                  pl.BlockSpec((1, tm, LANES), lambda bi, i: (bi, i, COL_KRR // LANES)),
                  _const_spec((1, Q_LORA)),
                  _const_spec((1, KV_LORA)),
                  _const_spec((Q_LORA, hq)),
                  _const_spec((KV_LORA, hq)),
                  pl.BlockSpec((tm, HEAD_PAD), lambda bi, i: (i, 0)),
                  pl.BlockSpec((tm, LANES), lambda bi, i: (i, 0))],
        out_specs=(pl.BlockSpec((1, tm, hq), lambda bi, i: (bi, i, 0)),
                   pl.BlockSpec((1, tm, hq), lambda bi, i: (bi, i, 0)),
                   pl.BlockSpec((1, tm, hv), lambda bi, i: (bi, i, 0))),
        compiler_params=_params("arbitrary", "arbitrary"),
        name="mla_latent",
    )(p1, p1, p1, q_norm.reshape(1, Q_LORA), kv_norm.reshape(1, KV_LORA), wq, wkv, tabq, tabk)


def _mla_ctx(pc, kv_norm, wkv, tm):
    b, c, _ = pc.shape
    hq = MLA_HEADS * HEAD_PAD
    hv = MLA_HEADS * HEAD_PAD
    return pl.pallas_call(
        _mla_ctx_kernel,
        out_shape=(jax.ShapeDtypeStruct((b, c, hq), BF16),
                   jax.ShapeDtypeStruct((b, c, hv), BF16)),
        grid=(b, c // tm),
        in_specs=[pl.BlockSpec((1, tm, KV_LORA), lambda bi, i: (bi, i, 0)),
                  pl.BlockSpec((1, tm, LANES), lambda bi, i: (bi, i, KV_LORA // LANES)),
                  _const_spec((1, KV_LORA)),
                  _const_spec((KV_LORA, hq))],
        out_specs=(pl.BlockSpec((1, tm, hq), lambda bi, i: (bi, i, 0)),
                   pl.BlockSpec((1, tm, hv), lambda bi, i: (bi, i, 0))),
        compiler_params=_params("arbitrary", "arbitrary"),
        name="mla_ctx",
    )(pc, pc, kv_norm.reshape(1, KV_LORA), wkv)


def _attn_kernel(q_ref, kl_ref, vl_ref, kc_ref, vc_ref, o_ref, *, chunk, heads):
    nt = (((1,), (1,)), ((), ()))
    seq = kl_ref.shape[1]
    segments = [(kl_ref, vl_ref, s0, chunk) for s0 in range(0, seq, chunk)] + [(kc_ref, vc_ref, 0, kc_ref.shape[1])]
    m = [None] * heads
    acc = [None] * heads
    for k_ref, v_ref, s0, size in segments:
        for h in range(heads):
            cols = slice(h * HEAD_PAD, (h + 1) * HEAD_PAD)
            s = lax.dot_general(q_ref[0, :, cols], k_ref[0, s0:s0 + size, cols], nt,
                                preferred_element_type=F32)
            m_seg = jnp.max(s, axis=-1, keepdims=True)
            m_new = m_seg if m[h] is None else jnp.maximum(m[h], m_seg)
            p = jnp.exp2(s - m_new).astype(BF16)
            pv = jnp.dot(p, v_ref[0, s0:s0 + size, cols], preferred_element_type=F32)
            acc[h] = pv if m[h] is None else acc[h] * jnp.exp2(m[h] - m_new) + pv
            m[h] = m_new
    for h in range(heads):
        o_ref[0, :, h * V_HEAD:(h + 1) * V_HEAD] = (acc[h][:, :V_HEAD] / acc[h][:, V_HEAD:V_HEAD + 1]).astype(o_ref.dtype)


def _attention(q, k_lat, v_lat, k_ctx, v_ctx, tq, chunk, heads):
    b, l, _ = q.shape
    c = k_ctx.shape[1]
    wide = heads * HEAD_PAD
    return pl.pallas_call(
        functools.partial(_attn_kernel, chunk=chunk, heads=heads),
        out_shape=jax.ShapeDtypeStruct((b, l, MLA_HEADS * V_HEAD), BF16),
        grid=(b, MLA_HEADS // heads, l // tq),
        in_specs=[pl.BlockSpec((1, tq, wide), lambda bi, h, i: (bi, i, h)),
                  pl.BlockSpec((1, l, wide), lambda bi, h, i: (bi, 0, h)),
                  pl.BlockSpec((1, l, wide), lambda bi, h, i: (bi, 0, h)),
                  pl.BlockSpec((1, c, wide), lambda bi, h, i: (bi, 0, h)),
                  pl.BlockSpec((1, c, wide), lambda bi, h, i: (bi, 0, h))],
        out_specs=pl.BlockSpec((1, tq, heads * V_HEAD), lambda bi, h, i: (bi, i, h)),
        compiler_params=_params("arbitrary", "arbitrary", "arbitrary"),
        name="attention",
    )(q, k_lat, v_lat, k_ctx, v_ctx)


def _merge_kernel(gp_ref, ga_ref, pool_ref, hb_ref, ha_ref, attn_ref, pw_ref, ps_ref, wpu_ref, wao_ref, o_ref,
                  *, tm, seq):
    i = pl.program_id(1)
    hb = hb_ref[0]
    ha = ha_ref[0]
    hb = jnp.where(i == 0, jnp.zeros_like(hb), hb)
    ha = jnp.where(i == pl.num_programs(1) - 1, jnp.zeros_like(ha), ha)
    u = pool_ref[0]
    uext = jnp.concatenate([hb, u, ha], axis=0)
    rows = lax.broadcasted_iota(jnp.int32, (tm, tm + 2 * HALO), 0)
    cols = lax.broadcasted_iota(jnp.int32, (tm, tm + 2 * HALO), 1)
    off = cols - HALO - rows
    t = i * tm + lax.broadcasted_iota(jnp.int32, (tm, 1), 0)
    mixed = []
    for g, w in enumerate(POOL_WINDOWS):
        lo, hi = g * POOL_GROUP_DIM, (g + 1) * POOL_GROUP_DIM
        band = jnp.where((off >= -(w // 2)) & (off < w // 2), 1.0, 0.0).astype(BF16)
        wsum = jnp.dot(band, uext[:, lo:hi], preferred_element_type=F32)
        cnt = (jnp.minimum(t + w // 2, seq) - jnp.maximum(t - w // 2, 0)).astype(F32)
        pooled = wsum / cnt - u[:, lo:hi].astype(F32)
        mixed.append(jnp.dot(pooled.astype(BF16), pw_ref[g], preferred_element_type=F32))
    pool_out = jnp.concatenate(mixed, axis=-1) * ps_ref[...]
    pool_up = jnp.dot(pool_out.astype(BF16), wpu_ref[...], preferred_element_type=F32)
    attn_o = jnp.dot(attn_ref[0], wao_ref[...], preferred_element_type=F32)
    merged = _sigmoid(gp_ref[0].astype(F32)) * pool_up + _sigmoid(ga_ref[0].astype(F32)) * attn_o
    o_ref[0] = merged.astype(o_ref.dtype)


def _merge(p1, attn, pool_w, pool_scale, w_pool_up, w_attn_o, tm):
    b, l, _ = p1.shape
    hb = tm // HALO
    nhalo = l // HALO
    return pl.pallas_call(
        functools.partial(_merge_kernel, tm=tm, seq=l),
        out_shape=jax.ShapeDtypeStruct((b, l, D_MODEL), BF16),
        grid=(b, l // tm),
        in_specs=[pl.BlockSpec((1, tm, D_MODEL), lambda bi, i: (bi, i, COL_GATE_POOL // D_MODEL)),
                  pl.BlockSpec((1, tm, D_MODEL), lambda bi, i: (bi, i, COL_GATE_ATTN // D_MODEL)),
                  pl.BlockSpec((1, tm, POOL_DIM), lambda bi, i: (bi, i, COL_POOL // POOL_DIM)),
                  pl.BlockSpec((1, HALO, POOL_DIM),
                               lambda bi, i: (bi, jnp.maximum(i * hb - 1, 0), COL_POOL // POOL_DIM)),
                  pl.BlockSpec((1, HALO, POOL_DIM),
                               lambda bi, i: (bi, jnp.minimum((i + 1) * hb, nhalo - 1), COL_POOL // POOL_DIM)),
                  pl.BlockSpec((1, tm, D_MODEL), lambda bi, i: (bi, i, 0)),
                  _const_spec((POOL_GROUPS, POOL_GROUP_DIM, POOL_GROUP_DIM)),
                  _const_spec((1, POOL_DIM)),
                  _const_spec((POOL_DIM, D_MODEL)),
                  _const_spec((D_MODEL, D_MODEL))],
        out_specs=pl.BlockSpec((1, tm, D_MODEL), lambda bi, i: (bi, i, 0)),
        compiler_params=_params("arbitrary", "arbitrary"),
        name="merge",
    )(p1, p1, p1, p1, p1, attn, pool_w, pool_scale.reshape(1, POOL_DIM), w_pool_up, w_attn_o)


ROUTE_E1, ROUTE_E2, ROUTE_W1, ROUTE_W2, ROUTE_R1, ROUTE_R2 = 0, 1, 2, 3, 4, 5
GROUP_LANE0 = N_EXPERTS
PACK = D_MODEL // 2
MOE_TILE = 256
MOE_TILES_MAX_EXTRA = N_EXPERTS
DMA_UNROLL = 8


def _route(logits, carry):
    tm = logits.shape[0]
    lane = lax.broadcasted_iota(jnp.int32, (tm, LANES), 1)
    lanef = lane.astype(F32)
    neg = jnp.float32(-jnp.inf)
    big = jnp.float32(1e9)
    is_g = (lane >= GROUP_LANE0) & (lane < GROUP_LANE0 + N_GROUPS)
    lg = jnp.where(is_g, logits, neg)
    mg = jnp.max(lg, axis=-1, keepdims=True)
    pg_top = 1.0 / jnp.sum(jnp.exp(lg - mg), axis=-1, keepdims=True)
    g_idx = jnp.min(jnp.where(lg == mg, lanef, big), axis=-1, keepdims=True) - GROUP_LANE0
    e_lo = g_idx * EXPERTS_PER_GROUP
    in_grp = (lanef >= e_lo) & (lanef < e_lo + EXPERTS_PER_GROUP)
    ls = jnp.where(in_grp, logits, neg)
    v1 = jnp.max(ls, axis=-1, keepdims=True)
    i1 = jnp.min(jnp.where(ls == v1, lanef, big), axis=-1, keepdims=True)
    ls2 = jnp.where(lanef == i1, neg, ls)
    v2 = jnp.max(ls2, axis=-1, keepdims=True)
    i2 = jnp.min(jnp.where(ls2 == v2, lanef, big), axis=-1, keepdims=True)
    den = jnp.sum(jnp.exp(ls - v1), axis=-1, keepdims=True)
    pe1 = 1.0 / den
    pe2 = jnp.exp(v2 - v1) / den
    w1 = pg_top * pe1 / (pe1 + pe2)
    w2 = pg_top * pe2 / (pe1 + pe2)
    sel = jnp.where((lanef == i1) | (lanef == i2), 1.0, 0.0)
    r = lax.broadcasted_iota(jnp.int32, (tm, tm), 0)
    cc = lax.broadcasted_iota(jnp.int32, (tm, tm), 1)
    before = jnp.where(cc < r, 1.0, 0.0).astype(BF16)
    seen = jnp.dot(before, sel.astype(BF16), preferred_element_type=F32) + carry
    r1 = jnp.sum(jnp.where(lanef == i1, seen, 0.0), axis=-1, keepdims=True)
    r2 = jnp.sum(jnp.where(lanef == i2, seen, 0.0), axis=-1, keepdims=True)
    rec = jnp.zeros((tm, LANES), F32)
    for pos, val in ((ROUTE_E1, i1), (ROUTE_E2, i2), (ROUTE_W1, w1), (ROUTE_W2, w2), (ROUTE_R1, r1), (ROUTE_R2, r2)):
        rec = jnp.where(lane == pos, val, rec)
    return rec, carry + jnp.sum(sel, axis=0, keepdims=True)


def _pack_bf16_pairs(h):
    bits = lax.bitcast_convert_type(h.astype(F32), jnp.uint32)
    return (bits[:, :PACK] >> 16) | bits[:, PACK:]


def _unpack_bf16_pairs(words):
    lo = lax.bitcast_convert_type(words << 16, F32)
    hi = lax.bitcast_convert_type(words & jnp.uint32(0xFFFF0000), F32)
    return jnp.concatenate([lo, hi], axis=-1).astype(BF16)


def _out_route_kernel(m_ref, x_ref, wo_ref, gpost_ref, g1_ref, gpre_ref, sh_ref, sc_ref, wrh_ref, wrl_ref, br_ref,
                      x1_ref, hw_ref, route_ref, cnt_ref, carry_ref):
    @pl.when((pl.program_id(0) == 0) & (pl.program_id(1) == 0))
    def _():
        carry_ref[...] = jnp.zeros_like(carry_ref)

    y = jnp.dot(m_ref[0], wo_ref[...], preferred_element_type=F32)
    x1 = x_ref[0] + g1_ref[0] * _rms(y, gpost_ref[...])
    x1_ref[0] = x1
    h2 = _rms(x1, gpre_ref[...]) * (1.0 + sc_ref[0]) + sh_ref[0]
    hi = h2.astype(BF16)
    lo = (h2 - hi.astype(F32)).astype(BF16)
    hw_ref[0] = _pack_bf16_pairs(hi)
    logits = (jnp.dot(hi, wrh_ref[...], preferred_element_type=F32)
              + jnp.dot(lo, wrh_ref[...], preferred_element_type=F32)
              + jnp.dot(hi, wrl_ref[...], preferred_element_type=F32)) + br_ref[...]
    rec, carry = _route(logits, carry_ref[...])
    route_ref[0] = rec
    carry_ref[...] = carry
    cnt_ref[...] = carry


def _out_route(merged, x, w_out, g_post, g1, g_pre, sh2, sc2, wr_hi, wr_lo, br, tm):
    b, l, _ = x.shape
    tile = lambda w: pl.BlockSpec((1, tm, w), lambda bi, i: (bi, i, 0))
    per_b = pl.BlockSpec((1, 1, D_MODEL), lambda bi, i: (bi, 0, 0))
    return pl.pallas_call(
        _out_route_kernel,
        out_shape=(jax.ShapeDtypeStruct((b, l, D_MODEL), F32),
                   jax.ShapeDtypeStruct((b, l, PACK), jnp.uint32),
                   jax.ShapeDtypeStruct((b, l, LANES), F32),
                   jax.ShapeDtypeStruct((1, LANES), F32)),
        grid=(b, l // tm),
        in_specs=[tile(D_MODEL), tile(D_MODEL),
                  _const_spec((D_MODEL, D_MODEL)),
                  _const_spec((1, D_MODEL)),
                  per_b,
                  _const_spec((1, D_MODEL)),
                  per_b, per_b,
                  _const_spec((D_MODEL, LANES)),
                  _const_spec((D_MODEL, LANES)),
                  _const_spec((1, LANES))],
        out_specs=(tile(D_MODEL), tile(PACK), tile(LANES), pl.BlockSpec((1, LANES), lambda bi, i: (0, 0))),
        scratch_shapes=[pltpu.VMEM((1, LANES), F32)],
        compiler_params=_params("arbitrary", "arbitrary"),
        name="out_route",
    )(merged, x, w_out, g_post.reshape(1, D_MODEL), g1, g_pre.reshape(1, D_MODEL), sh2, sc2, wr_hi, wr_lo, br)


def _row_copy(src_ref, src_row, dst_ref, dst_row, sem):
    return pltpu.make_async_copy(src_ref.at[pl.ds(src_row, 1)], dst_ref.at[pl.ds(dst_row, 1)], sem)


def _dispatch_kernel(pad_start_ref, pad_len_ref, n_tiles_ref, dest_ref, h_ref, xs_ref, zero_ref, sem, *, tm, n_max):
    i = pl.program_id(0)

    def tile_copy(j):
        return pltpu.make_async_copy(zero_ref, xs_ref.at[pl.ds(j * MOE_TILE, MOE_TILE)], sem)

    def start_token(r, carry):
        for k in range(2):
            _row_copy(h_ref, r, xs_ref, dest_ref[0, 0, 2 * r + k], sem).start()
        return carry

    def wait_token(r, carry):
        for k in range(2):
            _row_copy(h_ref, 0, xs_ref, 0, sem).wait()
        return carry

    lax.fori_loop(0, tm, start_token, 0, unroll=DMA_UNROLL)

    @pl.when(i == 0)
    def _():
        zero_ref[...] = jnp.zeros_like(zero_ref)

        def start_expert(e, carry):
            def start_row(j, c):
                _row_copy(zero_ref, 0, xs_ref, pad_start_ref[e] + j, sem).start()
                return c
            return lax.fori_loop(0, pad_len_ref[e], start_row, carry)

        def wait_expert(e, carry):
            def wait_row(j, c):
                _row_copy(zero_ref, 0, xs_ref, 0, sem).wait()
                return c
            return lax.fori_loop(0, pad_len_ref[e], wait_row, carry)

        lax.fori_loop(0, N_EXPERTS, start_expert, 0)
        lax.fori_loop(n_tiles_ref[0], n_max, lambda j, c: (tile_copy(j).start(), c)[1], 0)
        lax.fori_loop(0, N_EXPERTS, wait_expert, 0)
        lax.fori_loop(n_tiles_ref[0], n_max, lambda j, c: (tile_copy(j).wait(), c)[1], 0)

    lax.fori_loop(0, tm, wait_token, 0, unroll=DMA_UNROLL)


def _dispatch(hw, dest, pad_start, pad_len, n_tiles, n_max, tm):
    t = hw.shape[0]
    grid_spec = pltpu.PrefetchScalarGridSpec(
        num_scalar_prefetch=3,
        grid=(t // tm,),
        in_specs=[pl.BlockSpec((1, 1, 2 * tm), lambda i, *_: (i, 0, 0), memory_space=pltpu.SMEM),
                  pl.BlockSpec((tm, PACK), lambda i, *_: (i, 0))],
        out_specs=pl.BlockSpec(memory_space=pl.ANY),
        scratch_shapes=[pltpu.VMEM((MOE_TILE, PACK), jnp.uint32), pltpu.SemaphoreType.DMA(())],
    )
    return pl.pallas_call(
        functools.partial(_dispatch_kernel, tm=tm, n_max=n_max),
        out_shape=jax.ShapeDtypeStruct((n_max * MOE_TILE, PACK), jnp.uint32),
        grid_spec=grid_spec,
        compiler_params=_params("arbitrary"),
        name="dispatch",
    )(pad_start, pad_len, n_tiles, dest.reshape(t // tm, 1, 2 * tm), hw)


def _moe_kernel(tile_expert_ref, tile_first_ref, n_tiles_ref, x_ref, wg_ref, wu_ref, wd_ref, y_ref,
                wg_bf, wu_bf, wd_bf):
    j = pl.program_id(0)

    @pl.when(j < n_tiles_ref[0])
    def _():
        @pl.when(tile_first_ref[j] == 1)
        def _():
            wg_bf[...] = wg_ref[0].astype(BF16)
            wu_bf[...] = wu_ref[0].astype(BF16)
            wd_bf[...] = wd_ref[0].astype(BF16)

        h = _unpack_bf16_pairs(x_ref[...])
        gate = jnp.dot(h, wg_bf[...], preferred_element_type=F32)
        up = jnp.dot(h, wu_bf[...], preferred_element_type=F32)
        act = (gate * _sigmoid(gate)) * up
        y_ref[...] = jnp.dot(act.astype(BF16), wd_bf[...], preferred_element_type=F32)

    @pl.when(j >= n_tiles_ref[0])
    def _():
        y_ref[...] = jnp.zeros_like(y_ref)


def _moe(xs, tile_expert, tile_first, n_tiles, w_gate, w_up, w_down):
    n_max = xs.shape[0] // MOE_TILE
    row_map = lambda j, te, tf, nt: (jnp.minimum(j, nt[0] - 1), 0)
    out_map = lambda j, te, tf, nt: (j, 0)
    w_map = lambda j, te, tf, nt: (te[j], 0, 0)
    grid_spec = pltpu.PrefetchScalarGridSpec(
        num_scalar_prefetch=3,
        grid=(n_max,),
        in_specs=[pl.BlockSpec((MOE_TILE, PACK), row_map),
                  pl.BlockSpec((1, D_MODEL, D_EXPERT), w_map),
                  pl.BlockSpec((1, D_MODEL, D_EXPERT), w_map),
                  pl.BlockSpec((1, D_EXPERT, D_MODEL), w_map)],
        out_specs=pl.BlockSpec((MOE_TILE, D_MODEL), out_map),
        scratch_shapes=[pltpu.VMEM((D_MODEL, D_EXPERT), BF16), pltpu.VMEM((D_MODEL, D_EXPERT), BF16),
                        pltpu.VMEM((D_EXPERT, D_MODEL), BF16)],
    )
    return pl.pallas_call(
        _moe_kernel,
        out_shape=jax.ShapeDtypeStruct((xs.shape[0], D_MODEL), F32),
        grid_spec=grid_spec,
        compiler_params=_params("arbitrary"),
        name="moe",
    )(tile_expert, tile_first, n_tiles, xs, w_gate, w_up, w_down)


def _final_kernel(dest_ref, dest_next_ref, x_ref, route_ref, ys_ref, g_ref, g2_ref, o_ref, rows_ref, sems, *, tm):
    step = pl.program_id(0) * pl.num_programs(1) + pl.program_id(1)
    n_steps = pl.num_programs(0) * pl.num_programs(1)
    slot = step % 2

    def start_rows(d_ref, s):
        def body(r, carry):
            for k in range(2):
                _row_copy(ys_ref, d_ref[0, 0, 2 * r + k], rows_ref.at[s, k], r, sems.at[s]).start()
            return carry
        lax.fori_loop(0, tm, body, 0, unroll=DMA_UNROLL)

    def wait_rows(s):
        def body(r, carry):
            for k in range(2):
                _row_copy(ys_ref, 0, rows_ref.at[s, k], 0, sems.at[s]).wait()
            return carry
        lax.fori_loop(0, tm, body, 0, unroll=DMA_UNROLL)

    @pl.when(step == 0)
    def _():
        start_rows(dest_ref, 0)

    @pl.when(step + 1 < n_steps)
    def _():
        start_rows(dest_next_ref, 1 - slot)

    wait_rows(slot)
    route = route_ref[0]
    w1 = route[:, ROUTE_W1:ROUTE_W1 + 1]
    w2 = route[:, ROUTE_W2:ROUTE_W2 + 1]
    y = w1 * rows_ref[slot, 0] + w2 * rows_ref[slot, 1]
    o_ref[0] = x_ref[0] + g2_ref[0] * _rms(y, g_ref[...])


def _final(x1, route, ys, dest, g_post, g2, tm):
    b, l, _ = x1.shape
    nt = l // tm
    last = b * nt - 1
    grid_spec = pltpu.PrefetchScalarGridSpec(
        num_scalar_prefetch=0,
        grid=(b, nt),
        in_specs=[pl.BlockSpec((1, 1, 2 * tm), lambda bi, i: (bi * nt + i, 0, 0), memory_space=pltpu.SMEM),
                  pl.BlockSpec((1, 1, 2 * tm), lambda bi, i: (jnp.minimum(bi * nt + i + 1, last), 0, 0),
                               memory_space=pltpu.SMEM),
                  pl.BlockSpec((1, tm, D_MODEL), lambda bi, i: (bi, i, 0)),
                  pl.BlockSpec((1, tm, LANES), lambda bi, i: (bi, i, 0)),
                  pl.BlockSpec(memory_space=pl.ANY),
                  _const_spec((1, D_MODEL)),
                  pl.BlockSpec((1, 1, D_MODEL), lambda bi, i: (bi, 0, 0))],
        out_specs=pl.BlockSpec((1, tm, D_MODEL), lambda bi, i: (bi, i, 0)),
        scratch_shapes=[pltpu.VMEM((2, 2, tm, D_MODEL), F32), pltpu.SemaphoreType.DMA((2,))],
    )
    dest3 = dest.reshape(b * nt, 1, 2 * tm)
    return pl.pallas_call(
        functools.partial(_final_kernel, tm=tm),
        out_shape=jax.ShapeDtypeStruct((b, l, D_MODEL), F32),
        grid_spec=grid_spec,
        compiler_params=_params("arbitrary", "arbitrary"),
        name="final",
    )(dest3, dest3, x1, route, ys, g_post.reshape(1, D_MODEL), g2)


def _rope_perm():
    j = np.arange(QK_ROPE)
    first_half = (j % (2 * ROPE_FREQS)) < ROPE_FREQS
    idx = np.where(first_half, j + ROPE_FREQS, j - ROPE_FREQS)
    sign = np.where(first_half, -1.0, 1.0).astype(np.float32)
    return idx, sign


def _rope_tables(seq):
    rows = seq // GRID_W
    row = jnp.repeat(jnp.arange(rows, dtype=F32), GRID_W)
    col = jnp.tile(jnp.arange(GRID_W, dtype=F32), rows)
    freqs = 1.0 / (ROPE_BASE ** (jnp.arange(ROPE_FREQS, dtype=F32) / ROPE_FREQS))
    ang_r = row[:, None] * freqs[None, :]
    ang_c = col[:, None] * freqs[None, :]
    ang = jnp.concatenate([ang_r, ang_r, ang_c, ang_c], axis=-1)
    cos, sin = jnp.cos(ang), jnp.sin(ang)
    tabk = jnp.concatenate([cos, sin], axis=-1)
    qs = ATTN_SCALE * LOG2_E
    tabq = jnp.concatenate([jnp.full((seq, QK_NOPE), qs, F32), tabk * qs], axis=-1)
    return tabq, tabk


def _tile(n, pref):
    return pref if n % pref == 0 else n


def kernel(x, c, ctx, c_ctx, w_mod, b_mod, norm_pre_mix, norm_post_mix, norm_pre_ffn, norm_post_ffn, w_in, pool_w,
           pool_scale, w_pool_up, q_norm, w_uq, kv_norm, w_ukv, w_attn_o, w_out, router_group_w, router_group_b,
           router_expert_w, router_expert_b, w_gate, w_up, w_down):
    bsz, seq, _ = x.shape
    assert w_mod.shape[0] == 1, "single-layer block"
    idx, sign = _rope_perm()

    rows = -(-(bsz + 1) // 8) * 8
    cvec = jnp.concatenate([c, c_ctx[None, :], jnp.zeros((rows - bsz - 1, D_MODEL), F32)], axis=0)
    mod = _modulation(cvec, w_mod[0], b_mod[0])
    mod_x = mod[:bsz].reshape(bsz, N_MOD, 1, D_MODEL)
    sh1, sc1, g1, sh2, sc2, g2 = (mod_x[:, k] for k in range(N_MOD))
    mod_c = mod[bsz].reshape(N_MOD, 1, 1, D_MODEL)
    sh1c, sc1c = mod_c[0], mod_c[1]

    wi = w_in[0]
    s0, s1, s2, s3 = POOL_DIM, POOL_DIM + Q_LORA, POOL_DIM + Q_LORA + KV_LORA, POOL_DIM + Q_LORA + KV_LORA + QK_ROPE
    w_pool, w_qc, w_kvc, w_kr, w_gates = wi[:, :s0], wi[:, s0:s1], wi[:, s1:s2], wi[:, s2:s3], wi[:, s3:]
    w_kr_rot = w_kr[:, idx] * sign
    w1 = jnp.concatenate([w_gates, w_pool, w_qc, w_kvc, w_kr, w_kr_rot,
                          jnp.zeros((D_MODEL, IN_COLS_PAD - IN_COLS), F32)], axis=1).astype(BF16)
    w1c = jnp.concatenate([w_kvc, w_kr, w_kr], axis=1).astype(BF16)
    p1 = _in_proj(x, norm_pre_mix[0], sh1, sc1, w1, _tile(seq, 1024), IN_BLOCK_N)
    pc = _in_proj(ctx, norm_pre_mix[0], sh1c, sc1c, w1c, _tile(ctx.shape[1], 1024), w1c.shape[1])

    wq = w_uq[0].reshape(Q_LORA, MLA_HEADS, QK_HEAD)
    wq_rope = wq[:, :, QK_NOPE:]
    wq_ext = jnp.concatenate([wq, wq_rope[:, :, idx] * sign], axis=-1).reshape(Q_LORA, MLA_HEADS * HEAD_PAD)
    wkv = w_ukv[0].reshape(KV_LORA, MLA_HEADS, QK_NOPE + V_HEAD)
    wkv_re = jnp.concatenate([wkv[:, :, :QK_NOPE].reshape(KV_LORA, -1), wkv[:, :, QK_NOPE:].reshape(KV_LORA, -1)],
                             axis=1)
    wq_ext, wkv_re = wq_ext.astype(BF16), wkv_re.astype(BF16)
    tabq, tabk = _rope_tables(seq)
    q, k_lat, v_lat = _mla_latent(p1, q_norm[0], kv_norm[0], wq_ext, wkv_re, tabq, tabk, _tile(seq, 512))
    k_ctx, v_ctx = _mla_ctx(pc, kv_norm[0], wkv_re, _tile(ctx.shape[1], 256))

    attn = _attention(q, k_lat, v_lat, k_ctx, v_ctx, _tile(seq, ATTN_TQ), _tile(seq, ATTN_CHUNK), ATTN_HEADS)

    merged = _merge(p1, attn, pool_w[0].astype(BF16), pool_scale[0], w_pool_up[0].astype(BF16),
                    w_attn_o[0].astype(BF16), _tile(seq, 512))

    wr = jnp.concatenate([router_expert_w[0], router_group_w[0],
                          jnp.zeros((D_MODEL, LANES - N_EXPERTS - N_GROUPS), F32)], axis=1)
    wr_hi = wr.astype(BF16)
    wr_lo = (wr - wr_hi.astype(F32)).astype(BF16)
    br = jnp.concatenate([router_expert_b[0], router_group_b[0],
                          jnp.zeros((LANES - N_EXPERTS - N_GROUPS,), F32)]).reshape(1, LANES)
    x1, hw, route, counts = _out_route(merged, x, w_out[0].astype(BF16), norm_post_mix[0], g1, norm_pre_ffn[0],
                                       sh2, sc2, wr_hi, wr_lo, br, _tile(seq, 512))

    tokens = bsz * seq
    n_tiles_max = 2 * tokens // MOE_TILE + MOE_TILES_MAX_EXTRA
    cnt = counts[0, :N_EXPERTS].astype(jnp.int32)
    tiles_per = (cnt + MOE_TILE - 1) // MOE_TILE
    tile_end = jnp.cumsum(tiles_per)
    seg_start = (tile_end - tiles_per) * MOE_TILE
    n_tiles = tile_end[-1:]
    tile_id = jnp.minimum(jnp.arange(n_tiles_max, dtype=jnp.int32), n_tiles[0] - 1)
    tile_expert = jnp.sum((tile_id[:, None] >= tile_end[None, :]).astype(jnp.int32), axis=1)
    tile_first = jnp.concatenate([jnp.ones((1,), jnp.int32),
                                  (tile_expert[1:] != tile_expert[:-1]).astype(jnp.int32)])
    rt = route.reshape(tokens, LANES)
    experts = jnp.arange(N_EXPERTS, dtype=jnp.int32)[None, :]

    def dest_of(e_lane, r_lane):
        e = rt[:, e_lane].astype(jnp.int32)
        return rt[:, r_lane].astype(jnp.int32) + jnp.sum(jnp.where(e[:, None] == experts, seg_start[None, :], 0), axis=1)

    dest = jnp.stack([dest_of(ROUTE_E1, ROUTE_R1), dest_of(ROUTE_E2, ROUTE_R2)], axis=1)

    xs = _dispatch(hw.reshape(tokens, PACK), dest, seg_start + cnt, tiles_per * MOE_TILE - cnt,
                   n_tiles, n_tiles_max, _tile(tokens, 1024))
    ys = _moe(xs, tile_expert, tile_first, n_tiles, w_gate[0], w_up[0], w_down[0])
    return _final(x1, route, ys, dest, norm_post_ffn[0], g2, _tile(seq, 256))
```

```python
import functools

import jax
import jax.numpy as jnp
from jax import lax
from jax.experimental import pallas as pl
from jax.experimental.pallas import tpu as pltpu

F32 = jnp.float32
BF16 = jnp.bfloat16

D_MODEL = 2048
N_MOD = 6
EPS = 1e-6
GRID_W = 64
POOL_WINDOWS = (2, 4, 8, 16)
POOL_GROUPS = 4
POOL_GROUP_DIM = D_MODEL // 8
POOL_DIM = POOL_GROUPS * POOL_GROUP_DIM
MLA_HEADS = D_MODEL // 128
Q_LORA = D_MODEL // 4
KV_LORA = D_MODEL // 4
QK_NOPE = 128
QK_ROPE = 64
V_HEAD = 128
QK_HEAD = QK_NOPE + QK_ROPE
ROPE_FREQS = QK_ROPE // 4
ROPE_BASE = 10000.0
ATTN_SCALE = QK_HEAD ** -0.5
LOG2_E = 1.4426950408889634
N_GROUPS = 4
EXPERTS_PER_GROUP = 8
N_EXPERTS = N_GROUPS * EXPERTS_PER_GROUP
D_EXPERT = D_MODEL // 4

LANES = 128
HEAD_PAD = 2 * LANES
HALO = 16
ATTN_TQ = 1024
ATTN_CHUNK = 1024
ATTN_HEADS = 2
V7X_VMEM_BYTES = 64 * 1024 * 1024
VMEM_LIMIT = 56 * 1024 * 1024

COL_GATE_POOL = 0
COL_GATE_ATTN = D_MODEL
COL_POOL = 2 * D_MODEL
COL_QC = COL_POOL + POOL_DIM
COL_KVC = COL_QC + Q_LORA
COL_KRR = COL_KVC + KV_LORA
IN_COLS = COL_KRR + LANES
IN_COLS_PAD = 6400
IN_BLOCK_N = 1280


def _params(*sem):
    return pltpu.CompilerParams(dimension_semantics=sem, vmem_limit_bytes=VMEM_LIMIT)


def _sigmoid(v):
    return 1.0 / (1.0 + jnp.exp(-v))


def _rms(v, g):
    v = v.astype(F32)
    return v * lax.rsqrt(jnp.mean(v * v, axis=-1, keepdims=True) + EPS) * g


def _const_spec(shape):
    nd = len(shape)
    return pl.BlockSpec(shape, lambda *_: (0,) * nd, pipeline_mode=pl.Buffered(1))


def _mod_kernel(c_ref, w_ref, b_ref, o_ref):
    a = c_ref[...]
    a = a * _sigmoid(a)
    o_ref[...] = jnp.dot(a.astype(BF16), w_ref[...].astype(BF16), preferred_element_type=F32) + b_ref[...]


def _modulation(cvec, w_mod, b_mod):
    rows = cvec.shape[0]
    n = w_mod.shape[1]
    bn = 1536
    return pl.pallas_call(
        _mod_kernel,
        out_shape=jax.ShapeDtypeStruct((rows, n), F32),
        grid=(n // bn,),
        in_specs=[pl.BlockSpec((rows, D_MODEL), lambda j: (0, 0)),
                  pl.BlockSpec((D_MODEL, bn), lambda j: (0, j)),
                  pl.BlockSpec((1, bn), lambda j: (0, j))],
        out_specs=pl.BlockSpec((rows, bn), lambda j: (0, j)),
        compiler_params=_params("arbitrary"),
        name="mod",
    )(cvec, w_mod, b_mod.reshape(1, n))


def _inproj_kernel(x_ref, g_ref, sh_ref, sc_ref, w_ref, o_ref, h_ref):
    @pl.when(pl.program_id(2) == 0)
    def _():
        h = _rms(x_ref[0], g_ref[...]) * (1.0 + sc_ref[0]) + sh_ref[0]
        h_ref[...] = h.astype(BF16)

    o_ref[0] = jnp.dot(h_ref[...], w_ref[...], preferred_element_type=F32).astype(o_ref.dtype)


def _in_proj(x, gain, shift, scale, w, tm, bn):
    b, l, _ = x.shape
    n = w.shape[1]
    per_batch = shift.shape[0] == b
    mod_map = (lambda bi, i, j: (bi, 0, 0)) if per_batch else (lambda bi, i, j: (0, 0, 0))
    return pl.pallas_call(
        _inproj_kernel,
        out_shape=jax.ShapeDtypeStruct((b, l, n), BF16),
        grid=(b, l // tm, n // bn),
        in_specs=[pl.BlockSpec((1, tm, D_MODEL), lambda bi, i, j: (bi, i, 0)),
                  pl.BlockSpec((1, D_MODEL), lambda bi, i, j: (0, 0)),
                  pl.BlockSpec((1, 1, D_MODEL), mod_map),
                  pl.BlockSpec((1, 1, D_MODEL), mod_map),
                  pl.BlockSpec((D_MODEL, bn), lambda bi, i, j: (0, j))],
        out_specs=pl.BlockSpec((1, tm, bn), lambda bi, i, j: (bi, i, j)),
        scratch_shapes=[pltpu.VMEM((tm, D_MODEL), BF16)],
        compiler_params=_params("arbitrary", "arbitrary", "arbitrary"),
        name="in_proj",
    )(x, gain.reshape(1, D_MODEL), shift, scale, w)


def _mla_latent_kernel(qc_ref, kvc_ref, krr_ref, qn_ref, kvn_ref, wq_ref, wkv_ref, tq_ref, tk_ref,
                       q_out, k_out, v_out):
    qn = _rms(qc_ref[0], qn_ref[...]).astype(BF16)
    q = jnp.dot(qn, wq_ref[...], preferred_element_type=F32)
    tq = tq_ref[...]
    for h in range(MLA_HEADS):
        q_out[0, :, h * HEAD_PAD:(h + 1) * HEAD_PAD] = (q[:, h * HEAD_PAD:(h + 1) * HEAD_PAD] * tq).astype(BF16)
    kvn = _rms(kvc_ref[0], kvn_ref[...]).astype(BF16)
    kv = jnp.dot(kvn, wkv_ref[...], preferred_element_type=F32)
    u = krr_ref[0].astype(F32) * tk_ref[...]
    krot = (u + pltpu.roll(u, QK_ROPE, axis=1)).astype(BF16)
    _store_kv(kv, krot, k_out, v_out)


def _store_kv(kv, k_rope, k_out, v_out):
    tm = kv.shape[0]
    lane = lax.broadcasted_iota(jnp.int32, (tm, LANES), 1)
    ones_col = jnp.where(lane == 0, 1.0, 0.0).astype(BF16)
    v0 = MLA_HEADS * QK_NOPE
    for h in range(MLA_HEADS):
        k_out[0, :, h * HEAD_PAD:h * HEAD_PAD + QK_NOPE] = kv[:, h * QK_NOPE:(h + 1) * QK_NOPE].astype(BF16)
        k_out[0, :, h * HEAD_PAD + QK_NOPE:(h + 1) * HEAD_PAD] = k_rope
        v_out[0, :, h * HEAD_PAD:h * HEAD_PAD + V_HEAD] = kv[:, v0 + h * V_HEAD:v0 + (h + 1) * V_HEAD].astype(BF16)
        v_out[0, :, h * HEAD_PAD + V_HEAD:(h + 1) * HEAD_PAD] = ones_col


def _mla_ctx_kernel(kvc_ref, krr_ref, kvn_ref, wkv_ref, k_out, v_out):
    kvn = _rms(kvc_ref[0], kvn_ref[...]).astype(BF16)
    kv = jnp.dot(kvn, wkv_ref[...], preferred_element_type=F32)
    _store_kv(kv, krr_ref[0], k_out, v_out)


def _mla_latent(p1, q_norm, kv_norm, wq, wkv, tabq, tabk, tm):
    b, l, _ = p1.shape
    hq = MLA_HEADS * HEAD_PAD
    hv = MLA_HEADS * HEAD_PAD
    return pl.pallas_call(
        _mla_latent_kernel,
        out_shape=(jax.ShapeDtypeStruct((b, l, hq), BF16),
                   jax.ShapeDtypeStruct((b, l, hq), BF16),
                   jax.ShapeDtypeStruct((b, l, hv), BF16)),
        grid=(b, l // tm),
        in_specs=[pl.BlockSpec((1, tm, Q_LORA), lambda bi, i: (bi, i, COL_QC // Q_LORA)),
                  pl.BlockSpec((1, tm, KV_LORA), lambda bi, i: (bi, i, COL_KVC // KV_LORA)),
                  pl.BlockSpec((1, tm, LANES), lambda bi, i: (bi, i, COL_KRR // LANES)),
                  _const_spec((1, Q_LORA)),
                  _const_spec((1, KV_LORA)),
                  _const_spec((Q_LORA, hq)),
                  _const_spec((KV_LORA, hq)),
                  pl.BlockSpec((tm, HEAD_PAD), lambda bi, i: (i, 0)),
                  pl.BlockSpec((tm, LANES), lambda bi, i: (i, 0))],
        out_specs=(pl.BlockSpec((1, tm, hq), lambda bi, i: (bi, i, 0)),
                   pl.BlockSpec((1, tm, hq), lambda bi, i: (bi, i, 0)),
                   pl.BlockSpec((1, tm, hv), lambda bi, i: (bi, i, 0))),
        compiler_params=_params("arbitrary", "arbitrary"),
        name="mla_latent",
    )(p1, p1, p1, q_norm.reshape(1, Q_LORA), kv_norm.reshape(1, KV_LORA), wq, wkv, tabq, tabk)


def _mla_ctx(pc, kv_norm, wkv, tm):
    b, c, _ = pc.shape
    hq = MLA_HEADS * HEAD_PAD
    hv = MLA_HEADS * HEAD_PAD
    return pl.pallas_call(
        _mla_ctx_kernel,
        out_shape=(jax.ShapeDtypeStruct((b, c, hq), BF16),
                   jax.ShapeDtypeStruct((b, c, hv), BF16)),
        grid=(b, c // tm),
        in_specs=[pl.BlockSpec((1, tm, KV_LORA), lambda bi, i: (bi, i, 0)),
                  pl.BlockSpec((1, tm, LANES), lambda bi, i: (bi, i, KV_LORA // LANES)),
                  _const_spec((1, KV_LORA)),
                  _const_spec((KV_LORA, hq))],
        out_specs=(pl.BlockSpec((1, tm, hq), lambda bi, i: (bi, i, 0)),
                   pl.BlockSpec((1, tm, hv), lambda bi, i: (bi, i, 0))),
        compiler_params=_params("arbitrary", "arbitrary"),
        name="mla_ctx",
    )(pc, pc, kv_norm.reshape(1, KV_LORA), wkv)


def _attn_kernel(q_ref, kl_ref, vl_ref, kc_ref, vc_ref, o_ref, *, chunk, heads):
    nt = (((1,), (1,)), ((), ()))
    seq = kl_ref.shape[1]
    segments = [(kl_ref, vl_ref, s0, chunk) for s0 in range(0, seq, chunk)] + [(kc_ref, vc_ref, 0, kc_ref.shape[1])]
    m = [None] * heads
    acc = [None] * heads
    for k_ref, v_ref, s0, size in segments:
        for h in range(heads):
            cols = slice(h * HEAD_PAD, (h + 1) * HEAD_PAD)
            s = lax.dot_general(q_ref[0, :, cols], k_ref[0, s0:s0 + size, cols], nt,
                                preferred_element_type=F32)
            m_seg = jnp.max(s, axis=-1, keepdims=True)
            m_new = m_seg if m[h] is None else jnp.maximum(m[h], m_seg)
            p = jnp.exp2(s - m_new).astype(BF16)
            pv = jnp.dot(p, v_ref[0, s0:s0 + size, cols], preferred_element_type=F32)
            acc[h] = pv if m[h] is None else acc[h] * jnp.exp2(m[h] - m_new) + pv
            m[h] = m_new
    for h in range(heads):
        o_ref[0, :, h * V_HEAD:(h + 1) * V_HEAD] = (acc[h][:, :V_HEAD] / acc[h][:, V_HEAD:V_HEAD + 1]).astype(o_ref.dtype)


def _attention(q, k_lat, v_lat, k_ctx, v_ctx, tq, chunk, heads):
    b, l, _ = q.shape
    c = k_ctx.shape[1]
    wide = heads * HEAD_PAD
    return pl.pallas_call(
        functools.partial(_attn_kernel, chunk=chunk, heads=heads),
        out_shape=jax.ShapeDtypeStruct((b, l, MLA_HEADS * V_HEAD), BF16),
        grid=(b, MLA_HEADS // heads, l // tq),
        in_specs=[pl.BlockSpec((1, tq, wide), lambda bi, h, i: (bi, i, h)),
                  pl.BlockSpec((1, l, wide), lambda bi, h, i: (bi, 0, h)),
                  pl.BlockSpec((1, l, wide), lambda bi, h, i: (bi, 0, h)),
                  pl.BlockSpec((1, c, wide), lambda bi, h, i: (bi, 0, h)),
                  pl.BlockSpec((1, c, wide), lambda bi, h, i: (bi, 0, h))],
        out_specs=pl.BlockSpec((1, tq, heads * V_HEAD), lambda bi, h, i: (bi, i, h)),
        compiler_params=_params("arbitrary", "arbitrary", "arbitrary"),
        name="attention",
    )(q, k_lat, v_lat, k_ctx, v_ctx)


def _merge_kernel(gp_ref, ga_ref, pool_ref, hb_ref, ha_ref, attn_ref, pw_ref, ps_ref, wpu_ref, wao_ref, o_ref,
                  *, tm, seq):
    i = pl.program_id(1)
    hb = hb_ref[0]
    ha = ha_ref[0]
    hb = jnp.where(i == 0, jnp.zeros_like(hb), hb)
    ha = jnp.where(i == pl.num_programs(1) - 1, jnp.zeros_like(ha), ha)
    u = pool_ref[0]
    uext = jnp.concatenate([hb, u, ha], axis=0)
    rows = lax.broadcasted_iota(jnp.int32, (tm, tm + 2 * HALO), 0)
    cols = lax.broadcasted_iota(jnp.int32, (tm, tm + 2 * HALO), 1)
    off = cols - HALO - rows
    t = i * tm + lax.broadcasted_iota(jnp.int32, (tm, 1), 0)
    mixed = []
    for g, w in enumerate(POOL_WINDOWS):
        lo, hi = g * POOL_GROUP_DIM, (g + 1) * POOL_GROUP_DIM
        band = jnp.where((off >= -(w // 2)) & (off < w // 2), 1.0, 0.0).astype(BF16)
        wsum = jnp.dot(band, uext[:, lo:hi], preferred_element_type=F32)
        cnt = (jnp.minimum(t + w // 2, seq) - jnp.maximum(t - w // 2, 0)).astype(F32)
        pooled = wsum / cnt - u[:, lo:hi].astype(F32)
        mixed.append(jnp.dot(pooled.astype(BF16), pw_ref[g], preferred_element_type=F32))
    pool_out = jnp.concatenate(mixed, axis=-1) * ps_ref[...]
    pool_up = jnp.dot(pool_out.astype(BF16), wpu_ref[...], preferred_element_type=F32)
    attn_o = jnp.dot(attn_ref[0], wao_ref[...], preferred_element_type=F32)
    merged = _sigmoid(gp_ref[0].astype(F32)) * pool_up + _sigmoid(ga_ref[0].astype(F32)) * attn_o
    o_ref[0] = merged.astype(o_ref.dtype)


def _merge(p1, attn, pool_w, pool_scale, w_pool_up, w_attn_o, tm):
    b, l, _ = p1.shape
    hb = tm // HALO
    nhalo = l // HALO
    return pl.pallas_call(
        functools.partial(_merge_kernel, tm=tm, seq=l),
        out_shape=jax.ShapeDtypeStruct((b, l, D_MODEL), BF16),
        grid=(b, l // tm),
        in_specs=[pl.BlockSpec((1, tm, D_MODEL), lambda bi, i: (bi, i, COL_GATE_POOL // D_MODEL)),
                  pl.BlockSpec((1, tm, D_MODEL), lambda bi, i: (bi, i, COL_GATE_ATTN // D_MODEL)),
                  pl.BlockSpec((1, tm, POOL_DIM), lambda bi, i: (bi, i, COL_POOL // POOL_DIM)),
                  pl.BlockSpec((1, HALO, POOL_DIM),
                               lambda bi, i: (bi, jnp.maximum(i * hb - 1, 0), COL_POOL // POOL_DIM)),
                  pl.BlockSpec((1, HALO, POOL_DIM),
                               lambda bi, i: (bi, jnp.minimum((i + 1) * hb, nhalo - 1), COL_POOL // POOL_DIM)),
                  pl.BlockSpec((1, tm, D_MODEL), lambda bi, i: (bi, i, 0)),
                  _const_spec((POOL_GROUPS, POOL_GROUP_DIM, POOL_GROUP_DIM)),
                  _const_spec((1, POOL_DIM)),
                  _const_spec((POOL_DIM, D_MODEL)),
                  _const_spec((D_MODEL, D_MODEL))],
        out_specs=pl.BlockSpec((1, tm, D_MODEL), lambda bi, i: (bi, i, 0)),
        compiler_params=_params("arbitrary", "arbitrary"),
        name="merge",
    )(p1, p1, p1, p1, p1, attn, pool_w, pool_scale.reshape(1, POOL_DIM), w_pool_up, w_attn_o)


ROUTE_E1, ROUTE_E2, ROUTE_W1, ROUTE_W2, ROUTE_R1, ROUTE_R2 = 0, 1, 2, 3, 4, 5
GROUP_LANE0 = N_EXPERTS
MOE_TILE = 512
MOE_TILES_MAX_EXTRA = N_EXPERTS
DMA_UNROLL = 8

def _route(logits, carry):
    tm = logits.shape[0]
    lane = lax.broadcasted_iota(jnp.int32, (tm, LANES), 1)
    lanef = lane.astype(F32)
    neg = jnp.float32(-jnp.inf)
    big = jnp.float32(1e9)
    is_g = (lane >= GROUP_LANE0) & (lane < GROUP_LANE0 + N_GROUPS)
    lg = jnp.where(is_g, logits, neg)
    mg = jnp.max(lg, axis=-1, keepdims=True)
    pg_top = 1.0 / jnp.sum(jnp.exp(lg - mg), axis=-1, keepdims=True)
    g_idx = jnp.min(jnp.where(lg == mg, lanef, big), axis=-1, keepdims=True) - GROUP_LANE0
    e_lo = g_idx * EXPERTS_PER_GROUP
    in_grp = (lanef >= e_lo) & (lanef < e_lo + EXPERTS_PER_GROUP)
    ls = jnp.where(in_grp, logits, neg)
    v1 = jnp.max(ls, axis=-1, keepdims=True)
    i1 = jnp.min(jnp.where(ls == v1, lanef, big), axis=-1, keepdims=True)
    ls2 = jnp.where(lanef == i1, neg, ls)
    v2 = jnp.max(ls2, axis=-1, keepdims=True)
    i2 = jnp.min(jnp.where(ls2 == v2, lanef, big), axis=-1, keepdims=True)
    den = jnp.sum(jnp.exp(ls - v1), axis=-1, keepdims=True)
    pe1 = 1.0 / den
    pe2 = jnp.exp(v2 - v1) / den
    w1 = pg_top * pe1 / (pe1 + pe2)
    w2 = pg_top * pe2 / (pe1 + pe2)
    sel = jnp.where((lanef == i1) | (lanef == i2), 1.0, 0.0)
    r = lax.broadcasted_iota(jnp.int32, (tm, tm), 0)
    cc = lax.broadcasted_iota(jnp.int32, (tm, tm), 1)
    before = jnp.where(cc < r, 1.0, 0.0).astype(BF16)
    seen = jnp.dot(before, sel.astype(BF16), preferred_element_type=F32) + carry
    r1 = jnp.sum(jnp.where(lanef == i1, seen, 0.0), axis=-1, keepdims=True)
    r2 = jnp.sum(jnp.where(lanef == i2, seen, 0.0), axis=-1, keepdims=True)
    rec = jnp.zeros((tm, LANES), F32)
    for pos, val in ((ROUTE_E1, i1), (ROUTE_E2, i2), (ROUTE_W1, w1), (ROUTE_W2, w2), (ROUTE_R1, r1), (ROUTE_R2, r2)):
        rec = jnp.where(lane == pos, val, rec)
    return rec, carry + jnp.sum(sel, axis=0, keepdims=True)


def _out_route_kernel(m_ref, x_ref, wo_ref, gpost_ref, g1_ref, gpre_ref, sh_ref, sc_ref, wrh_ref, wrl_ref, br_ref,
                      x1_ref, hw_ref, route_ref, cnt_ref, carry_ref):
    @pl.when((pl.program_id(0) == 0) & (pl.program_id(1) == 0))
    def _():
        carry_ref[...] = jnp.zeros_like(carry_ref)

    y = jnp.dot(m_ref[0], wo_ref[...], preferred_element_type=F32)
    x1 = x_ref[0] + g1_ref[0] * _rms(y, gpost_ref[...])
    x1_ref[0] = x1
    h2 = _rms(x1, gpre_ref[...]) * (1.0 + sc_ref[0]) + sh_ref[0]
    hi = h2.astype(BF16)
    lo = (h2 - hi.astype(F32)).astype(BF16)
    hw_ref[0] = h2
    logits = (jnp.dot(hi, wrh_ref[...], preferred_element_type=F32)
              + jnp.dot(lo, wrh_ref[...], preferred_element_type=F32)
              + jnp.dot(hi, wrl_ref[...], preferred_element_type=F32)) + br_ref[...]
    rec, carry = _route(logits, carry_ref[...])
    route_ref[0] = rec
    carry_ref[...] = carry
    cnt_ref[...] = carry


def _out_route(merged, x, w_out, g_post, g1, g_pre, sh2, sc2, wr_hi, wr_lo, br, tm):
    b, l, _ = x.shape
    tile = lambda w: pl.BlockSpec((1, tm, w), lambda bi, i: (bi, i, 0))
    per_b = pl.BlockSpec((1, 1, D_MODEL), lambda bi, i: (bi, 0, 0))
    return pl.pallas_call(
        _out_route_kernel,
        out_shape=(jax.ShapeDtypeStruct((b, l, D_MODEL), F32),
                   jax.ShapeDtypeStruct((b, l, D_MODEL), F32),
                   jax.ShapeDtypeStruct((b, l, LANES), F32),
                   jax.ShapeDtypeStruct((1, LANES), F32)),
        grid=(b, l // tm),
        in_specs=[tile(D_MODEL), tile(D_MODEL),
                  _const_spec((D_MODEL, D_MODEL)),
                  _const_spec((1, D_MODEL)),
                  per_b,
                  _const_spec((1, D_MODEL)),
                  per_b, per_b,
                  _const_spec((D_MODEL, LANES)),
                  _const_spec((D_MODEL, LANES)),
                  _const_spec((1, LANES))],
        out_specs=(tile(D_MODEL), tile(D_MODEL), tile(LANES), pl.BlockSpec((1, LANES), lambda bi, i: (0, 0))),
        scratch_shapes=[pltpu.VMEM((1, LANES), F32)],
        compiler_params=_params("arbitrary", "arbitrary"),
        name="out_route",
    )(merged, x, w_out, g_post.reshape(1, D_MODEL), g1, g_pre.reshape(1, D_MODEL), sh2, sc2, wr_hi, wr_lo, br)


def _row_copy(src_ref, src_row, dst_ref, dst_row, sem):
    return pltpu.make_async_copy(src_ref.at[pl.ds(src_row, 1)], dst_ref.at[pl.ds(dst_row, 1)], sem)


def _dispatch_kernel(pad_start_ref, pad_len_ref, n_tiles_ref, dest_ref, h_ref, xs_ref, zero_ref, sem, *, tm, n_max):
    i = pl.program_id(0)

    def tile_copy(j):
        return pltpu.make_async_copy(zero_ref, xs_ref.at[pl.ds(j * MOE_TILE, MOE_TILE)], sem)

    def start_token(r, carry):
        for k in range(2):
            _row_copy(h_ref, r, xs_ref, dest_ref[0, 0, k * tm + r], sem).start(priority=k)
        return carry

    def wait_token(r, carry):
        for k in range(2):
            _row_copy(h_ref, 0, xs_ref, 0, sem).wait()
        return carry

    lax.fori_loop(0, tm, start_token, 0, unroll=DMA_UNROLL)

    @pl.when(i == 0)
    def _():
        zero_ref[...] = jnp.zeros_like(zero_ref)

        def start_expert(e, carry):
            def start_row(j, c):
                _row_copy(zero_ref, 0, xs_ref, pad_start_ref[e] + j, sem).start()
                return c
            return lax.fori_loop(0, pad_len_ref[e], start_row, carry)

        def wait_expert(e, carry):
            def wait_row(j, c):
                _row_copy(zero_ref, 0, xs_ref, 0, sem).wait()
                return c
            return lax.fori_loop(0, pad_len_ref[e], wait_row, carry)

        lax.fori_loop(0, N_EXPERTS, start_expert, 0)
        lax.fori_loop(n_tiles_ref[0], n_max, lambda j, c: (tile_copy(j).start(), c)[1], 0)
        lax.fori_loop(0, N_EXPERTS, wait_expert, 0)
        lax.fori_loop(n_tiles_ref[0], n_max, lambda j, c: (tile_copy(j).wait(), c)[1], 0)

    lax.fori_loop(0, tm, wait_token, 0, unroll=DMA_UNROLL)


def _dispatch(hw, dest, pad_start, pad_len, n_tiles, n_max, tm):
    t = hw.shape[0]
    grid_spec = pltpu.PrefetchScalarGridSpec(
        num_scalar_prefetch=3,
        grid=(t // tm,),
        in_specs=[pl.BlockSpec((1, 1, 2 * tm), lambda i, *_: (i, 0, 0), memory_space=pltpu.SMEM),
                  pl.BlockSpec((tm, D_MODEL), lambda i, *_: (i, 0))],
        out_specs=pl.BlockSpec(memory_space=pl.ANY),
        scratch_shapes=[pltpu.VMEM((MOE_TILE, D_MODEL), F32), pltpu.SemaphoreType.DMA(())],
    )
    return pl.pallas_call(
        functools.partial(_dispatch_kernel, tm=tm, n_max=n_max),
        out_shape=jax.ShapeDtypeStruct((n_max * MOE_TILE, D_MODEL), F32),
        grid_spec=grid_spec,
        compiler_params=_params("arbitrary"),
        name="dispatch",
    )(pad_start, pad_len, n_tiles, _dest_blocks(dest, tm), hw)


def _moe_kernel(tile_expert_ref, tile_first_ref, n_tiles_ref, x_ref, wg_ref, wu_ref, wd_ref, y_ref,
                wg_bf, wu_bf, wd_bf):
    j = pl.program_id(0)

    @pl.when(j < n_tiles_ref[0])
    def _():
        @pl.when(tile_first_ref[j] == 1)
        def _():
            wg_bf[...] = wg_ref[0].astype(BF16)
            wu_bf[...] = wu_ref[0].astype(BF16)
            wd_bf[...] = wd_ref[0].astype(BF16)

        h = x_ref[...].astype(BF16)
        gate = jnp.dot(h, wg_bf[...], preferred_element_type=F32)
        up = jnp.dot(h, wu_bf[...], preferred_element_type=F32)
        act = (gate * _sigmoid(gate)) * up
        y_ref[...] = jnp.dot(act.astype(BF16), wd_bf[...], preferred_element_type=F32)

    @pl.when(j >= n_tiles_ref[0])
    def _():
        y_ref[...] = jnp.zeros_like(y_ref)


def _moe(xs, tile_expert, tile_first, n_tiles, w_gate, w_up, w_down):
    n_max = xs.shape[0] // MOE_TILE
    row_map = lambda j, te, tf, nt: (jnp.minimum(j, nt[0] - 1), 0)
    out_map = lambda j, te, tf, nt: (j, 0)
    w_map = lambda j, te, tf, nt: (te[j], 0, 0)
    grid_spec = pltpu.PrefetchScalarGridSpec(
        num_scalar_prefetch=3,
        grid=(n_max,),
        in_specs=[pl.BlockSpec((MOE_TILE, D_MODEL), row_map),
                  pl.BlockSpec((1, D_MODEL, D_EXPERT), w_map),
                  pl.BlockSpec((1, D_MODEL, D_EXPERT), w_map),
                  pl.BlockSpec((1, D_EXPERT, D_MODEL), w_map)],
        out_specs=pl.BlockSpec((MOE_TILE, D_MODEL), out_map),
        scratch_shapes=[pltpu.VMEM((D_MODEL, D_EXPERT), BF16), pltpu.VMEM((D_MODEL, D_EXPERT), BF16),
                        pltpu.VMEM((D_EXPERT, D_MODEL), BF16)],
    )
    return pl.pallas_call(
        _moe_kernel,
        out_shape=jax.ShapeDtypeStruct((xs.shape[0], D_MODEL), F32),
        grid_spec=grid_spec,
        compiler_params=_params("arbitrary"),
        name="moe",
    )(tile_expert, tile_first, n_tiles, xs, w_gate, w_up, w_down)


def _final_kernel(dest_ref, dest_next_ref, x_ref, route_ref, ys_ref, g_ref, g2_ref, o_ref, rows_ref, sems, *, tm):
    step = pl.program_id(0) * pl.num_programs(1) + pl.program_id(1)
    n_steps = pl.num_programs(0) * pl.num_programs(1)
    slot = step % 2

    def start_rows(d_ref, s):
        def body(r, carry):
            for k in range(2):
                _row_copy(ys_ref, d_ref[0, 0, k * tm + r], rows_ref.at[s, k], r, sems.at[s]).start(priority=k)
            return carry
        lax.fori_loop(0, tm, body, 0, unroll=DMA_UNROLL)

    def wait_rows(s):
        def body(r, carry):
            for k in range(2):
                _row_copy(ys_ref, 0, rows_ref.at[s, k], 0, sems.at[s]).wait()
            return carry
        lax.fori_loop(0, tm, body, 0, unroll=DMA_UNROLL)

    @pl.when(step == 0)
    def _():
        start_rows(dest_ref, 0)

    @pl.when(step + 1 < n_steps)
    def _():
        start_rows(dest_next_ref, 1 - slot)

    wait_rows(slot)
    route = route_ref[0]
    w1 = route[:, ROUTE_W1:ROUTE_W1 + 1]
    w2 = route[:, ROUTE_W2:ROUTE_W2 + 1]
    y = w1 * rows_ref[slot, 0] + w2 * rows_ref[slot, 1]
    o_ref[0] = x_ref[0] + g2_ref[0] * _rms(y, g_ref[...])


def _final(x1, route, ys, dest, g_post, g2, tm):
    b, l, _ = x1.shape
    nt = l // tm
    last = b * nt - 1
    grid_spec = pltpu.PrefetchScalarGridSpec(
        num_scalar_prefetch=0,
        grid=(b, nt),
        in_specs=[pl.BlockSpec((1, 1, 2 * tm), lambda bi, i: (bi * nt + i, 0, 0), memory_space=pltpu.SMEM),
                  pl.BlockSpec((1, 1, 2 * tm), lambda bi, i: (jnp.minimum(bi * nt + i + 1, last), 0, 0),
                               memory_space=pltpu.SMEM),
                  pl.BlockSpec((1, tm, D_MODEL), lambda bi, i: (bi, i, 0)),
                  pl.BlockSpec((1, tm, LANES), lambda bi, i: (bi, i, 0)),
                  pl.BlockSpec(memory_space=pl.ANY),
                  _const_spec((1, D_MODEL)),
                  pl.BlockSpec((1, 1, D_MODEL), lambda bi, i: (bi, 0, 0))],
        out_specs=pl.BlockSpec((1, tm, D_MODEL), lambda bi, i: (bi, i, 0)),
        scratch_shapes=[pltpu.VMEM((2, 2, tm, D_MODEL), F32), pltpu.SemaphoreType.DMA((2,))],
    )
    dest3 = _dest_blocks(dest, tm)
    return pl.pallas_call(
        functools.partial(_final_kernel, tm=tm),
        out_shape=jax.ShapeDtypeStruct((b, l, D_MODEL), F32),
        grid_spec=grid_spec,
        compiler_params=_params("arbitrary", "arbitrary"),
        name="final",
    )(dest3, dest3, x1, route, ys, g_post.reshape(1, D_MODEL), g2)


def _rot_half(w):
    f = ROPE_FREQS
    parts = []
    for seg in range(QK_ROPE // (2 * f)):
        x0 = w[..., 2 * f * seg:2 * f * seg + f]
        x1 = w[..., 2 * f * seg + f:2 * f * (seg + 1)]
        parts += [-x1, x0]
    return jnp.concatenate(parts, axis=-1)


def _rope_tables(seq):
    rows = seq // GRID_W
    row = jnp.repeat(jnp.arange(rows, dtype=F32), GRID_W)
    col = jnp.tile(jnp.arange(GRID_W, dtype=F32), rows)
    freqs = 1.0 / (ROPE_BASE ** (jnp.arange(ROPE_FREQS, dtype=F32) / ROPE_FREQS))
    ang_r = row[:, None] * freqs[None, :]
    ang_c = col[:, None] * freqs[None, :]
    ang = jnp.concatenate([ang_r, ang_r, ang_c, ang_c], axis=-1)
    cos, sin = jnp.cos(ang), jnp.sin(ang)
    tabk = jnp.concatenate([cos, sin], axis=-1)
    qs = ATTN_SCALE * LOG2_E
    tabq = jnp.concatenate([jnp.full((seq, QK_NOPE), qs, F32), tabk * qs], axis=-1)
    return tabq, tabk


def _tile(n, pref):
    return pref if n % pref == 0 else n


def kernel(x, c, ctx, c_ctx, w_mod, b_mod, norm_pre_mix, norm_post_mix, norm_pre_ffn, norm_post_ffn, w_in, pool_w,
           pool_scale, w_pool_up, q_norm, w_uq, kv_norm, w_ukv, w_attn_o, w_out, router_group_w, router_group_b,
           router_expert_w, router_expert_b, w_gate, w_up, w_down):
    bsz, seq, _ = x.shape
    assert w_mod.shape[0] == 1, "single-layer block"

    rows = -(-(bsz + 1) // 8) * 8
    cvec = jnp.concatenate([c, c_ctx[None, :], jnp.zeros((rows - bsz - 1, D_MODEL), F32)], axis=0)
    mod = _modulation(cvec, w_mod[0], b_mod[0])
    mod_x = mod[:bsz].reshape(bsz, N_MOD, 1, D_MODEL)
    sh1, sc1, g1, sh2, sc2, g2 = (mod_x[:, k] for k in range(N_MOD))
    mod_c = mod[bsz].reshape(N_MOD, 1, 1, D_MODEL)
    sh1c, sc1c = mod_c[0], mod_c[1]

    wi = w_in[0].astype(BF16)
    s0, s1, s2, s3 = POOL_DIM, POOL_DIM + Q_LORA, POOL_DIM + Q_LORA + KV_LORA, POOL_DIM + Q_LORA + KV_LORA + QK_ROPE
    w_pool, w_qc, w_kvc, w_kr, w_gates = wi[:, :s0], wi[:, s0:s1], wi[:, s1:s2], wi[:, s2:s3], wi[:, s3:]
    w1 = jnp.concatenate([w_gates, w_pool, w_qc, w_kvc, w_kr, _rot_half(w_kr),
                          jnp.zeros((D_MODEL, IN_COLS_PAD - IN_COLS), BF16)], axis=1)
    w1c = jnp.concatenate([w_kvc, w_kr, w_kr], axis=1)
    p1 = _in_proj(x, norm_pre_mix[0], sh1, sc1, w1, _tile(seq, 1024), IN_BLOCK_N)
    pc = _in_proj(ctx, norm_pre_mix[0], sh1c, sc1c, w1c, _tile(ctx.shape[1], 1024), w1c.shape[1])

    wq = w_uq[0].astype(BF16).reshape(Q_LORA, MLA_HEADS, QK_HEAD)
    wq_ext = jnp.concatenate([wq, _rot_half(wq[:, :, QK_NOPE:])], axis=-1).reshape(Q_LORA, MLA_HEADS * HEAD_PAD)
    wkv = w_ukv[0].astype(BF16).reshape(KV_LORA, MLA_HEADS, QK_NOPE + V_HEAD)
    wkv_re = jnp.concatenate([wkv[:, :, :QK_NOPE].reshape(KV_LORA, -1), wkv[:, :, QK_NOPE:].reshape(KV_LORA, -1)],
                             axis=1)
    tabq, tabk = _rope_tables(seq)
    q, k_lat, v_lat = _mla_latent(p1, q_norm[0], kv_norm[0], wq_ext, wkv_re, tabq, tabk, _tile(seq, 512))
    k_ctx, v_ctx = _mla_ctx(pc, kv_norm[0], wkv_re, _tile(ctx.shape[1], 256))

    attn = _attention(q, k_lat, v_lat, k_ctx, v_ctx, _tile(seq, ATTN_TQ), _tile(seq, ATTN_CHUNK), ATTN_HEADS)

    merged = _merge(p1, attn, pool_w[0].astype(BF16), pool_scale[0], w_pool_up[0].astype(BF16),
                    w_attn_o[0].astype(BF16), _tile(seq, 512))

    wr = jnp.concatenate([router_expert_w[0], router_group_w[0],
                          jnp.zeros((D_MODEL, LANES - N_EXPERTS - N_GROUPS), F32)], axis=1)
    wr_hi = wr.astype(BF16)
    wr_lo = (wr - wr_hi.astype(F32)).astype(BF16)
    br = jnp.concatenate([router_expert_b[0], router_group_b[0],
                          jnp.zeros((LANES - N_EXPERTS - N_GROUPS,), F32)]).reshape(1, LANES)
    x1, hw, route, counts = _out_route(merged, x, w_out[0].astype(BF16), norm_post_mix[0], g1, norm_pre_ffn[0],
                                       sh2, sc2, wr_hi, wr_lo, br, _tile(seq, 512))

    tokens = bsz * seq
    n_tiles_max = 2 * tokens // MOE_TILE + MOE_TILES_MAX_EXTRA
    cnt = counts[0, :N_EXPERTS].astype(jnp.int32)
    tiles_per = (cnt + MOE_TILE - 1) // MOE_TILE
    tile_end = jnp.cumsum(tiles_per)
    seg_start = (tile_end - tiles_per) * MOE_TILE
    n_tiles = tile_end[-1:]
    tile_id = jnp.minimum(jnp.arange(n_tiles_max, dtype=jnp.int32), n_tiles[0] - 1)
    tile_expert = jnp.sum((tile_id[:, None] >= tile_end[None, :]).astype(jnp.int32), axis=1)
    tile_first = jnp.concatenate([jnp.ones((1,), jnp.int32),
                                  (tile_expert[1:] != tile_expert[:-1]).astype(jnp.int32)])
    rt = route.reshape(tokens, LANES)[:, :8].T.astype(jnp.int32)
    experts = jnp.arange(N_EXPERTS, dtype=jnp.int32)[:, None]

    def dest_of(e_lane, r_lane):
        seg = jnp.sum(jnp.where(rt[e_lane][None, :] == experts, seg_start[:, None], 0), axis=0)
        return rt[r_lane] + seg

    dest = (dest_of(ROUTE_E1, ROUTE_R1), dest_of(ROUTE_E2, ROUTE_R2))

    xs = _dispatch(hw.reshape(tokens, D_MODEL), dest, seg_start + cnt, tiles_per * MOE_TILE - cnt,
                   n_tiles, n_tiles_max, _tile(tokens, 1024))
    ys = _moe(xs, tile_expert, tile_first, n_tiles, w_gate[0], w_up[0], w_down[0])
    return _final(x1, route, ys, dest, norm_post_ffn[0], g2, _tile(seq, 256))


def _dest_blocks(dest, tm):
    d1, d2 = dest
    n = d1.shape[0] // tm
    return jnp.concatenate([d1.reshape(n, 1, tm), d2.reshape(n, 1, tm)], axis=2)
```

```python
import functools

import jax
import jax.numpy as jnp
from jax import lax
from jax.experimental import pallas as pl
from jax.experimental.pallas import tpu as pltpu

F32 = jnp.float32
BF16 = jnp.bfloat16

D_MODEL = 2048
N_MOD = 6
EPS = 1e-6
GRID_W = 64
POOL_WINDOWS = (2, 4, 8, 16)
POOL_GROUPS = 4
POOL_GROUP_DIM = D_MODEL // 8
POOL_DIM = POOL_GROUPS * POOL_GROUP_DIM
MLA_HEADS = D_MODEL // 128
Q_LORA = D_MODEL // 4
KV_LORA = D_MODEL // 4
QK_NOPE = 128
QK_ROPE = 64
V_HEAD = 128
QK_HEAD = QK_NOPE + QK_ROPE
ROPE_FREQS = QK_ROPE // 4
ROPE_BASE = 10000.0
ATTN_SCALE = QK_HEAD ** -0.5
LOG2_E = 1.4426950408889634
N_GROUPS = 4
EXPERTS_PER_GROUP = 8
N_EXPERTS = N_GROUPS * EXPERTS_PER_GROUP
D_EXPERT = D_MODEL // 4

LANES = 128
HEAD_PAD = 2 * LANES
HALO = 16
ATTN_TQ = 1024
ATTN_CHUNK = 1024
ATTN_HEADS = 2
V7X_VMEM_BYTES = 64 * 1024 * 1024
VMEM_LIMIT = 56 * 1024 * 1024

COL_GATE_POOL = 0
COL_GATE_ATTN = D_MODEL
COL_POOL = 2 * D_MODEL
COL_QC = COL_POOL + POOL_DIM
COL_KVC = COL_QC + Q_LORA
COL_KRR = COL_KVC + KV_LORA
IN_COLS = COL_KRR + LANES
IN_COLS_PAD = 6400
IN_BLOCK_N = 1280


def _params(*sem):
    return pltpu.CompilerParams(dimension_semantics=sem, vmem_limit_bytes=VMEM_LIMIT)


def _sigmoid(v):
    return 1.0 / (1.0 + jnp.exp(-v))


def _rms(v, g):
    v = v.astype(F32)
    return v * lax.rsqrt(jnp.mean(v * v, axis=-1, keepdims=True) + EPS) * g


def _const_spec(shape):
    nd = len(shape)
    return pl.BlockSpec(shape, lambda *_: (0,) * nd, pipeline_mode=pl.Buffered(1))


def _mod_kernel(c_ref, w_ref, b_ref, o_ref):
    a = c_ref[...]
    a = a * _sigmoid(a)
    o_ref[...] = jnp.dot(a.astype(BF16), w_ref[...].astype(BF16), preferred_element_type=F32) + b_ref[...]


def _modulation(cvec, w_mod, b_mod):
    rows = cvec.shape[0]
    n = w_mod.shape[1]
    bn = 1536
    return pl.pallas_call(
        _mod_kernel,
        out_shape=jax.ShapeDtypeStruct((rows, n), F32),
        grid=(n // bn,),
        in_specs=[pl.BlockSpec((rows, D_MODEL), lambda j: (0, 0)),
                  pl.BlockSpec((D_MODEL, bn), lambda j: (0, j)),
                  pl.BlockSpec((1, bn), lambda j: (0, j))],
        out_specs=pl.BlockSpec((rows, bn), lambda j: (0, j)),
        compiler_params=_params("arbitrary"),
        name="mod",
    )(cvec, w_mod, b_mod.reshape(1, n))


def _inproj_kernel(x_ref, g_ref, sh_ref, sc_ref, w_ref, o_ref, h_ref):
    @pl.when(pl.program_id(2) == 0)
    def _():
        h = _rms(x_ref[0], g_ref[...]) * (1.0 + sc_ref[0]) + sh_ref[0]
        h_ref[...] = h.astype(BF16)

    o_ref[0] = jnp.dot(h_ref[...], w_ref[...], preferred_element_type=F32).astype(o_ref.dtype)


def _in_proj(x, gain, shift, scale, w, tm, bn):
    b, l, _ = x.shape
    n = w.shape[1]
    per_batch = shift.shape[0] == b
    mod_map = (lambda bi, i, j: (bi, 0, 0)) if per_batch else (lambda bi, i, j: (0, 0, 0))
    return pl.pallas_call(
        _inproj_kernel,
        out_shape=jax.ShapeDtypeStruct((b, l, n), BF16),
        grid=(b, l // tm, n // bn),
        in_specs=[pl.BlockSpec((1, tm, D_MODEL), lambda bi, i, j: (bi, i, 0)),
                  pl.BlockSpec((1, D_MODEL), lambda bi, i, j: (0, 0)),
                  pl.BlockSpec((1, 1, D_MODEL), mod_map),
                  pl.BlockSpec((1, 1, D_MODEL), mod_map),
                  pl.BlockSpec((D_MODEL, bn), lambda bi, i, j: (0, j))],
        out_specs=pl.BlockSpec((1, tm, bn), lambda bi, i, j: (bi, i, j)),
        scratch_shapes=[pltpu.VMEM((tm, D_MODEL), BF16)],
        compiler_params=_params("arbitrary", "arbitrary", "arbitrary"),
        name="in_proj",
    )(x, gain.reshape(1, D_MODEL), shift, scale, w)


def _mla_latent_kernel(qc_ref, kvc_ref, krr_ref, qn_ref, kvn_ref, wq_ref, wkv_ref, tq_ref, tk_ref,
                       q_out, k_out, v_out):
    qn = _rms(qc_ref[0], qn_ref[...]).astype(BF16)
    q = jnp.dot(qn, wq_ref[...], preferred_element_type=F32)
    tq = tq_ref[...]
    for h in range(MLA_HEADS):
        q_out[0, :, h * HEAD_PAD:(h + 1) * HEAD_PAD] = (q[:, h * HEAD_PAD:(h + 1) * HEAD_PAD] * tq).astype(BF16)
    kvn = _rms(kvc_ref[0], kvn_ref[...]).astype(BF16)
    kv = jnp.dot(kvn, wkv_ref[...], preferred_element_type=F32)
    u = krr_ref[0].astype(F32) * tk_ref[...]
    krot = (u + pltpu.roll(u, QK_ROPE, axis=1)).astype(BF16)
    _store_kv(kv, krot, k_out, v_out)


def _store_kv(kv, k_rope, k_out, v_out):
    tm = kv.shape[0]
    lane = lax.broadcasted_iota(jnp.int32, (tm, LANES), 1)
    ones_col = jnp.where(lane == 0, 1.0, 0.0).astype(BF16)
    v0 = MLA_HEADS * QK_NOPE
    for h in range(MLA_HEADS):
        k_out[0, :, h * HEAD_PAD:h * HEAD_PAD + QK_NOPE] = kv[:, h * QK_NOPE:(h + 1) * QK_NOPE].astype(BF16)
        k_out[0, :, h * HEAD_PAD + QK_NOPE:(h + 1) * HEAD_PAD] = k_rope
        v_out[0, :, h * HEAD_PAD:h * HEAD_PAD + V_HEAD] = kv[:, v0 + h * V_HEAD:v0 + (h + 1) * V_HEAD].astype(BF16)
        v_out[0, :, h * HEAD_PAD + V_HEAD:(h + 1) * HEAD_PAD] = ones_col


def _mla_ctx_kernel(kvc_ref, krr_ref, kvn_ref, wkv_ref, k_out, v_out):
    kvn = _rms(kvc_ref[0], kvn_ref[...]).astype(BF16)
    kv = jnp.dot(kvn, wkv_ref[...], preferred_element_type=F32)
    _store_kv(kv, krr_ref[0], k_out, v_out)


def _mla_latent(p1, q_norm, kv_norm, wq, wkv, tabq, tabk, tm):
    b, l, _ = p1.shape
    hq = MLA_HEADS * HEAD_PAD
    hv = MLA_HEADS * HEAD_PAD
    return pl.pallas_call(
        _mla_latent_kernel,
        out_shape=(jax.ShapeDtypeStruct((b, l, hq), BF16),
                   jax.ShapeDtypeStruct((b, l, hq), BF16),
                   jax.ShapeDtypeStruct((b, l, hv), BF16)),
        grid=(b, l // tm),
        in_specs=[pl.BlockSpec((1, tm, Q_LORA), lambda bi, i: (bi, i, COL_QC // Q_LORA)),
                  pl.BlockSpec((1, tm, KV_LORA), lambda bi, i: (bi, i, COL_KVC // KV_LORA)),
                  pl.BlockSpec((1, tm, LANES), lambda bi, i: (bi, i, COL_KRR // LANES)),
                  _const_spec((1, Q_LORA)),
                  _const_spec((1, KV_LORA)),
                  _const_spec((Q_LORA, hq)),
                  _const_spec((KV_LORA, hq)),
                  pl.BlockSpec((tm, HEAD_PAD), lambda bi, i: (i, 0)),
                  pl.BlockSpec((tm, LANES), lambda bi, i: (i, 0))],
        out_specs=(pl.BlockSpec((1, tm, hq), lambda bi, i: (bi, i, 0)),
                   pl.BlockSpec((1, tm, hq), lambda bi, i: (bi, i, 0)),
                   pl.BlockSpec((1, tm, hv), lambda bi, i: (bi, i, 0))),
        compiler_params=_params("arbitrary", "arbitrary"),
        name="mla_latent",
    )(p1, p1, p1, q_norm.reshape(1, Q_LORA), kv_norm.reshape(1, KV_LORA), wq, wkv, tabq, tabk)


def _mla_ctx(pc, kv_norm, wkv, tm):
    b, c, _ = pc.shape
    hq = MLA_HEADS * HEAD_PAD
    hv = MLA_HEADS * HEAD_PAD
    return pl.pallas_call(
        _mla_ctx_kernel,
        out_shape=(jax.ShapeDtypeStruct((b, c, hq), BF16),
                   jax.ShapeDtypeStruct((b, c, hv), BF16)),
        grid=(b, c // tm),
        in_specs=[pl.BlockSpec((1, tm, KV_LORA), lambda bi, i: (bi, i, 0)),
                  pl.BlockSpec((1, tm, LANES), lambda bi, i: (bi, i, KV_LORA // LANES)),
                  _const_spec((1, KV_LORA)),
                  _const_spec((KV_LORA, hq))],
        out_specs=(pl.BlockSpec((1, tm, hq), lambda bi, i: (bi, i, 0)),
                   pl.BlockSpec((1, tm, hv), lambda bi, i: (bi, i, 0))),
        compiler_params=_params("arbitrary", "arbitrary"),
        name="mla_ctx",
    )(pc, pc, kv_norm.reshape(1, KV_LORA), wkv)


def _attn_kernel(q_ref, kl_ref, vl_ref, kc_ref, vc_ref, o_ref, *, chunk, heads):
    nt = (((1,), (1,)), ((), ()))
    seq = kl_ref.shape[1]
    segments = [(kl_ref, vl_ref, s0, chunk) for s0 in range(0, seq, chunk)] + [(kc_ref, vc_ref, 0, kc_ref.shape[1])]
    m = [None] * heads
    acc = [None] * heads
    for k_ref, v_ref, s0, size in segments:
        for h in range(heads):
            cols = slice(h * HEAD_PAD, (h + 1) * HEAD_PAD)
            s = lax.dot_general(q_ref[0, :, cols], k_ref[0, s0:s0 + size, cols], nt,
                                preferred_element_type=F32)
            m_seg = jnp.max(s, axis=-1, keepdims=True)
            m_new = m_seg if m[h] is None else jnp.maximum(m[h], m_seg)
            p = jnp.exp2(s - m_new).astype(BF16)
            pv = jnp.dot(p, v_ref[0, s0:s0 + size, cols], preferred_element_type=F32)
            acc[h] = pv if m[h] is None else acc[h] * jnp.exp2(m[h] - m_new) + pv
            m[h] = m_new
    for h in range(heads):
        o_ref[0, :, h * V_HEAD:(h + 1) * V_HEAD] = (acc[h][:, :V_HEAD] / acc[h][:, V_HEAD:V_HEAD + 1]).astype(o_ref.dtype)


def _attention(q, k_lat, v_lat, k_ctx, v_ctx, tq, chunk, heads):
    b, l, _ = q.shape
    c = k_ctx.shape[1]
    wide = heads * HEAD_PAD
    return pl.pallas_call(
        functools.partial(_attn_kernel, chunk=chunk, heads=heads),
        out_shape=jax.ShapeDtypeStruct((b, l, MLA_HEADS * V_HEAD), BF16),
        grid=(b, MLA_HEADS // heads, l // tq),
        in_specs=[pl.BlockSpec((1, tq, wide), lambda bi, h, i: (bi, i, h)),
                  pl.BlockSpec((1, l, wide), lambda bi, h, i: (bi, 0, h)),
                  pl.BlockSpec((1, l, wide), lambda bi, h, i: (bi, 0, h)),
                  pl.BlockSpec((1, c, wide), lambda bi, h, i: (bi, 0, h)),
                  pl.BlockSpec((1, c, wide), lambda bi, h, i: (bi, 0, h))],
        out_specs=pl.BlockSpec((1, tq, heads * V_HEAD), lambda bi, h, i: (bi, i, h)),
        compiler_params=_params("arbitrary", "arbitrary", "arbitrary"),
        name="attention",
    )(q, k_lat, v_lat, k_ctx, v_ctx)


def _merge_kernel(gp_ref, ga_ref, pool_ref, hb_ref, ha_ref, attn_ref, pw_ref, ps_ref, wpu_ref, wao_ref, o_ref,
                  *, tm, seq):
    i = pl.program_id(1)
    hb = hb_ref[0]
    ha = ha_ref[0]
    hb = jnp.where(i == 0, jnp.zeros_like(hb), hb)
    ha = jnp.where(i == pl.num_programs(1) - 1, jnp.zeros_like(ha), ha)
    u = pool_ref[0]
    uext = jnp.concatenate([hb, u, ha], axis=0)
    rows = lax.broadcasted_iota(jnp.int32, (tm, tm + 2 * HALO), 0)
    cols = lax.broadcasted_iota(jnp.int32, (tm, tm + 2 * HALO), 1)
    off = cols - HALO - rows
    t = i * tm + lax.broadcasted_iota(jnp.int32, (tm, 1), 0)
    mixed = []
    for g, w in enumerate(POOL_WINDOWS):
        lo, hi = g * POOL_GROUP_DIM, (g + 1) * POOL_GROUP_DIM
        band = jnp.where((off >= -(w // 2)) & (off < w // 2), 1.0, 0.0).astype(BF16)
        wsum = jnp.dot(band, uext[:, lo:hi], preferred_element_type=F32)
        cnt = (jnp.minimum(t + w // 2, seq) - jnp.maximum(t - w // 2, 0)).astype(F32)
        pooled = wsum / cnt - u[:, lo:hi].astype(F32)
        mixed.append(jnp.dot(pooled.astype(BF16), pw_ref[g], preferred_element_type=F32))
    pool_out = jnp.concatenate(mixed, axis=-1) * ps_ref[...]
    pool_up = jnp.dot(pool_out.astype(BF16), wpu_ref[...], preferred_element_type=F32)
    attn_o = jnp.dot(attn_ref[0], wao_ref[...], preferred_element_type=F32)
    merged = _sigmoid(gp_ref[0].astype(F32)) * pool_up + _sigmoid(ga_ref[0].astype(F32)) * attn_o
    o_ref[0] = merged.astype(o_ref.dtype)


def _merge(p1, attn, pool_w, pool_scale, w_pool_up, w_attn_o, tm):
    b, l, _ = p1.shape
    hb = tm // HALO
    nhalo = l // HALO
    return pl.pallas_call(
        functools.partial(_merge_kernel, tm=tm, seq=l),
        out_shape=jax.ShapeDtypeStruct((b, l, D_MODEL), BF16),
        grid=(b, l // tm),
        in_specs=[pl.BlockSpec((1, tm, D_MODEL), lambda bi, i: (bi, i, COL_GATE_POOL // D_MODEL)),
                  pl.BlockSpec((1, tm, D_MODEL), lambda bi, i: (bi, i, COL_GATE_ATTN // D_MODEL)),
                  pl.BlockSpec((1, tm, POOL_DIM), lambda bi, i: (bi, i, COL_POOL // POOL_DIM)),
                  pl.BlockSpec((1, HALO, POOL_DIM),
                               lambda bi, i: (bi, jnp.maximum(i * hb - 1, 0), COL_POOL // POOL_DIM)),
                  pl.BlockSpec((1, HALO, POOL_DIM),
                               lambda bi, i: (bi, jnp.minimum((i + 1) * hb, nhalo - 1), COL_POOL // POOL_DIM)),
                  pl.BlockSpec((1, tm, D_MODEL), lambda bi, i: (bi, i, 0)),
                  _const_spec((POOL_GROUPS, POOL_GROUP_DIM, POOL_GROUP_DIM)),
                  _const_spec((1, POOL_DIM)),
                  _const_spec((POOL_DIM, D_MODEL)),
                  _const_spec((D_MODEL, D_MODEL))],
        out_specs=pl.BlockSpec((1, tm, D_MODEL), lambda bi, i: (bi, i, 0)),
        compiler_params=_params("arbitrary", "arbitrary"),
        name="merge",
    )(p1, p1, p1, p1, p1, attn, pool_w, pool_scale.reshape(1, POOL_DIM), w_pool_up, w_attn_o)


ROUTE_E1, ROUTE_E2, ROUTE_W1, ROUTE_W2, ROUTE_R1, ROUTE_R2 = 0, 1, 2, 3, 4, 5
GROUP_LANE0 = N_EXPERTS
MOE_TILE = 512
MOE_TILES_MAX_EXTRA = N_EXPERTS
DMA_UNROLL = 8

def _route(logits, carry):
    tm = logits.shape[0]
    lane = lax.broadcasted_iota(jnp.int32, (tm, LANES), 1)
    lanef = lane.astype(F32)
    neg = jnp.float32(-jnp.inf)
    big = jnp.float32(1e9)
    is_g = (lane >= GROUP_LANE0) & (lane < GROUP_LANE0 + N_GROUPS)
    lg = jnp.where(is_g, logits, neg)
    mg = jnp.max(lg, axis=-1, keepdims=True)
    pg_top = 1.0 / jnp.sum(jnp.exp(lg - mg), axis=-1, keepdims=True)
    g_idx = jnp.min(jnp.where(lg == mg, lanef, big), axis=-1, keepdims=True) - GROUP_LANE0
    e_lo = g_idx * EXPERTS_PER_GROUP
    in_grp = (lanef >= e_lo) & (lanef < e_lo + EXPERTS_PER_GROUP)
    ls = jnp.where(in_grp, logits, neg)
    v1 = jnp.max(ls, axis=-1, keepdims=True)
    i1 = jnp.min(jnp.where(ls == v1, lanef, big), axis=-1, keepdims=True)
    ls2 = jnp.where(lanef == i1, neg, ls)
    v2 = jnp.max(ls2, axis=-1, keepdims=True)
    i2 = jnp.min(jnp.where(ls2 == v2, lanef, big), axis=-1, keepdims=True)
    den = jnp.sum(jnp.exp(ls - v1), axis=-1, keepdims=True)
    pe1 = 1.0 / den
    pe2 = jnp.exp(v2 - v1) / den
    w1 = pg_top * pe1 / (pe1 + pe2)
    w2 = pg_top * pe2 / (pe1 + pe2)
    sel = jnp.where((lanef == i1) | (lanef == i2), 1.0, 0.0)
    r = lax.broadcasted_iota(jnp.int32, (tm, tm), 0)
    cc = lax.broadcasted_iota(jnp.int32, (tm, tm), 1)
    before = jnp.where(cc < r, 1.0, 0.0).astype(BF16)
    seen = jnp.dot(before, sel.astype(BF16), preferred_element_type=F32) + carry
    r1 = jnp.sum(jnp.where(lanef == i1, seen, 0.0), axis=-1, keepdims=True)
    r2 = jnp.sum(jnp.where(lanef == i2, seen, 0.0), axis=-1, keepdims=True)
    rec = jnp.zeros((tm, LANES), F32)
    for pos, val in ((ROUTE_E1, i1), (ROUTE_E2, i2), (ROUTE_W1, w1), (ROUTE_W2, w2), (ROUTE_R1, r1), (ROUTE_R2, r2)):
        rec = jnp.where(lane == pos, val, rec)
    return rec, carry + jnp.sum(sel, axis=0, keepdims=True)


def _out_route_kernel(m_ref, x_ref, wo_ref, gpost_ref, g1_ref, gpre_ref, sh_ref, sc_ref, wrh_ref, wrl_ref, br_ref,
                      x1_ref, hw_ref, route_ref, cnt_ref, carry_ref):
    @pl.when((pl.program_id(0) == 0) & (pl.program_id(1) == 0))
    def _():
        carry_ref[...] = jnp.zeros_like(carry_ref)

    y = jnp.dot(m_ref[0], wo_ref[...], preferred_element_type=F32)
    x1 = x_ref[0] + g1_ref[0] * _rms(y, gpost_ref[...])
    x1_ref[0] = x1
    h2 = _rms(x1, gpre_ref[...]) * (1.0 + sc_ref[0]) + sh_ref[0]
    hi = h2.astype(BF16)
    lo = (h2 - hi.astype(F32)).astype(BF16)
    hw_ref[0] = h2
    both = jnp.dot(hi, wrl_ref[...], preferred_element_type=F32)
    logits = (both[:, :LANES] + both[:, LANES:]
              + jnp.dot(lo, wrh_ref[...], preferred_element_type=F32)) + br_ref[...]
    rec, carry = _route(logits, carry_ref[...])
    route_ref[0] = rec
    carry_ref[...] = carry
    cnt_ref[...] = carry


def _out_route(merged, x, w_out, g_post, g1, g_pre, sh2, sc2, wr_hi, wr_lo, br, tm):
    b, l, _ = x.shape
    tile = lambda w: pl.BlockSpec((1, tm, w), lambda bi, i: (bi, i, 0))
    per_b = pl.BlockSpec((1, 1, D_MODEL), lambda bi, i: (bi, 0, 0))
    return pl.pallas_call(
        _out_route_kernel,
        out_shape=(jax.ShapeDtypeStruct((b, l, D_MODEL), F32),
                   jax.ShapeDtypeStruct((b, l, D_MODEL), F32),
                   jax.ShapeDtypeStruct((b, l, LANES), F32),
                   jax.ShapeDtypeStruct((1, LANES), F32)),
        grid=(b, l // tm),
        in_specs=[tile(D_MODEL), tile(D_MODEL),
                  _const_spec((D_MODEL, D_MODEL)),
                  _const_spec((1, D_MODEL)),
                  per_b,
                  _const_spec((1, D_MODEL)),
                  per_b, per_b,
                  _const_spec((D_MODEL, LANES)),
                  _const_spec((D_MODEL, 2 * LANES)),
                  _const_spec((1, LANES))],
        out_specs=(tile(D_MODEL), tile(D_MODEL), tile(LANES), pl.BlockSpec((1, LANES), lambda bi, i: (0, 0))),
        scratch_shapes=[pltpu.VMEM((1, LANES), F32)],
        compiler_params=_params("arbitrary", "arbitrary"),
        name="out_route",
    )(merged, x, w_out, g_post.reshape(1, D_MODEL), g1, g_pre.reshape(1, D_MODEL), sh2, sc2, wr_hi, wr_lo, br)


def _row_copy(src_ref, src_row, dst_ref, dst_row, sem):
    return pltpu.make_async_copy(src_ref.at[pl.ds(src_row, 1)], dst_ref.at[pl.ds(dst_row, 1)], sem)


def _dispatch_kernel(pad_start_ref, pad_len_ref, n_tiles_ref, dest_ref, h_ref, xs_ref, zero_ref, sem, *, tm, n_max):
    i = pl.program_id(0)

    def tile_copy(j):
        return pltpu.make_async_copy(zero_ref, xs_ref.at[pl.ds(j * MOE_TILE, MOE_TILE)], sem)

    def start_token(r, carry):
        for k in range(2):
            _row_copy(h_ref, r, xs_ref, dest_ref[0, 0, k * tm + r], sem).start(priority=k)
        return carry

    def wait_token(r, carry):
        for k in range(2):
            _row_copy(h_ref, 0, xs_ref, 0, sem).wait()
        return carry

    lax.fori_loop(0, tm, start_token, 0, unroll=DMA_UNROLL)

    @pl.when(i == 0)
    def _():
        zero_ref[...] = jnp.zeros_like(zero_ref)

        def start_expert(e, carry):
            def start_row(j, c):
                _row_copy(zero_ref, 0, xs_ref, pad_start_ref[e] + j, sem).start()
                return c
            return lax.fori_loop(0, pad_len_ref[e], start_row, carry)

        def wait_expert(e, carry):
            def wait_row(j, c):
                _row_copy(zero_ref, 0, xs_ref, 0, sem).wait()
                return c
            return lax.fori_loop(0, pad_len_ref[e], wait_row, carry)

        lax.fori_loop(0, N_EXPERTS, start_expert, 0)
        lax.fori_loop(n_tiles_ref[0], n_max, lambda j, c: (tile_copy(j).start(), c)[1], 0)
        lax.fori_loop(0, N_EXPERTS, wait_expert, 0)
        lax.fori_loop(n_tiles_ref[0], n_max, lambda j, c: (tile_copy(j).wait(), c)[1], 0)

    lax.fori_loop(0, tm, wait_token, 0, unroll=DMA_UNROLL)


def _dispatch(hw, dest, pad_start, pad_len, n_tiles, n_max, tm):
    t = hw.shape[0]
    grid_spec = pltpu.PrefetchScalarGridSpec(
        num_scalar_prefetch=3,
        grid=(t // tm,),
        in_specs=[pl.BlockSpec((1, 1, 2 * tm), lambda i, *_: (i, 0, 0), memory_space=pltpu.SMEM),
                  pl.BlockSpec((tm, D_MODEL), lambda i, *_: (i, 0))],
        out_specs=pl.BlockSpec(memory_space=pl.ANY),
        scratch_shapes=[pltpu.VMEM((MOE_TILE, D_MODEL), F32), pltpu.SemaphoreType.DMA(())],
    )
    return pl.pallas_call(
        functools.partial(_dispatch_kernel, tm=tm, n_max=n_max),
        out_shape=jax.ShapeDtypeStruct((n_max * MOE_TILE, D_MODEL), F32),
        grid_spec=grid_spec,
        compiler_params=_params("arbitrary"),
        name="dispatch",
    )(pad_start, pad_len, n_tiles, _dest_blocks(dest, tm), hw)


def _moe_kernel(first_tile_ref, tiles_per_ref, n_tiles_ref, wg_ref, wu_ref, wd_ref, xs_ref, ys_ref,
                wg_bf, wu_bf, wd_bf, xbuf, ybuf, in_sems, out_sems, *, n_max):
    e = pl.program_id(0)
    nt = tiles_per_ref[e]
    t0 = first_tile_ref[e]

    def rows(t):
        return pl.ds(pl.multiple_of((t0 + t) * MOE_TILE, MOE_TILE), MOE_TILE)

    def in_copy(t, slot):
        return pltpu.make_async_copy(xs_ref.at[rows(t)], xbuf.at[slot], in_sems.at[slot])

    def out_copy(t, slot):
        return pltpu.make_async_copy(ybuf.at[slot], ys_ref.at[rows(t)], out_sems.at[slot])

    @pl.when(nt > 0)
    def _():
        in_copy(0, 0).start()
        wg_bf[...] = wg_ref[0].astype(BF16)
        wu_bf[...] = wu_ref[0].astype(BF16)
        wd_bf[...] = wd_ref[0].astype(BF16)

        def tile_step(t, carry):
            slot = t % 2
            in_copy(t, slot).wait()

            @pl.when(t + 1 < nt)
            def _():
                in_copy(t + 1, 1 - slot).start()

            @pl.when(t >= 2)
            def _():
                out_copy(t - 2, slot).wait()

            h = xbuf[slot].astype(BF16)
            gate = jnp.dot(h, wg_bf[...], preferred_element_type=F32)
            up = jnp.dot(h, wu_bf[...], preferred_element_type=F32)
            act = (gate * _sigmoid(gate)) * up
            ybuf[slot] = jnp.dot(act.astype(BF16), wd_bf[...], preferred_element_type=F32)
            out_copy(t, slot).start()
            return carry

        lax.fori_loop(0, nt, tile_step, 0)

        @pl.when(nt >= 2)
        def _():
            out_copy(nt - 2, nt % 2).wait()

        out_copy(nt - 1, (nt - 1) % 2).wait()

    @pl.when(e == pl.num_programs(0) - 1)
    def _():
        ybuf[0] = jnp.zeros(ybuf.shape[1:], F32)

        def tail_copy(j):
            return pltpu.make_async_copy(ybuf.at[0], ys_ref.at[pl.ds(pl.multiple_of(j * MOE_TILE, MOE_TILE), MOE_TILE)],
                                         out_sems.at[0])

        lax.fori_loop(n_tiles_ref[0], n_max, lambda j, c: (tail_copy(j).start(), c)[1], 0)
        lax.fori_loop(n_tiles_ref[0], n_max, lambda j, c: (tail_copy(j).wait(), c)[1], 0)


def _moe(xs, first_tile, tiles_per, n_tiles, w_gate, w_up, w_down):
    n_max = xs.shape[0] // MOE_TILE
    w_map = lambda e, *_: (e, 0, 0)
    grid_spec = pltpu.PrefetchScalarGridSpec(
        num_scalar_prefetch=3,
        grid=(N_EXPERTS,),
        in_specs=[pl.BlockSpec((1, D_MODEL, D_EXPERT), w_map),
                  pl.BlockSpec((1, D_MODEL, D_EXPERT), w_map),
                  pl.BlockSpec((1, D_EXPERT, D_MODEL), w_map),
                  pl.BlockSpec(memory_space=pl.ANY)],
        out_specs=pl.BlockSpec(memory_space=pl.ANY),
        scratch_shapes=[pltpu.VMEM((D_MODEL, D_EXPERT), BF16), pltpu.VMEM((D_MODEL, D_EXPERT), BF16),
                        pltpu.VMEM((D_EXPERT, D_MODEL), BF16),
                        pltpu.VMEM((2, MOE_TILE, D_MODEL), F32), pltpu.VMEM((2, MOE_TILE, D_MODEL), F32),
                        pltpu.SemaphoreType.DMA((2,)), pltpu.SemaphoreType.DMA((2,))],
    )
    return pl.pallas_call(
        functools.partial(_moe_kernel, n_max=n_max),
        out_shape=jax.ShapeDtypeStruct((xs.shape[0], D_MODEL), F32),
        grid_spec=grid_spec,
        compiler_params=_params("arbitrary"),
        name="moe",
    )(first_tile, tiles_per, n_tiles, w_gate, w_up, w_down, xs)


def _final_kernel(dest_ref, dest_next_ref, x_ref, route_ref, ys_ref, g_ref, g2_ref, o_ref, rows_ref, sems, *, tm):
    step = pl.program_id(0) * pl.num_programs(1) + pl.program_id(1)
    n_steps = pl.num_programs(0) * pl.num_programs(1)
    slot = step % 2

    def start_rows(d_ref, s):
        def body(r, carry):
            for k in range(2):
                _row_copy(ys_ref, d_ref[0, 0, k * tm + r], rows_ref.at[s, k], r, sems.at[s]).start(priority=k)
            return carry
        lax.fori_loop(0, tm, body, 0, unroll=DMA_UNROLL)

    def wait_rows(s):
        def body(r, carry):
            for k in range(2):
                _row_copy(ys_ref, 0, rows_ref.at[s, k], 0, sems.at[s]).wait()
            return carry
        lax.fori_loop(0, tm, body, 0, unroll=DMA_UNROLL)

    @pl.when(step == 0)
    def _():
        start_rows(dest_ref, 0)

    @pl.when(step + 1 < n_steps)
    def _():
        start_rows(dest_next_ref, 1 - slot)

    wait_rows(slot)
    route = route_ref[0]
    w1 = route[:, ROUTE_W1:ROUTE_W1 + 1]
    w2 = route[:, ROUTE_W2:ROUTE_W2 + 1]
    y = w1 * rows_ref[slot, 0] + w2 * rows_ref[slot, 1]
    o_ref[0] = x_ref[0] + g2_ref[0] * _rms(y, g_ref[...])


def _final(x1, route, ys, dest, g_post, g2, tm):
    b, l, _ = x1.shape
    nt = l // tm
    last = b * nt - 1
    grid_spec = pltpu.PrefetchScalarGridSpec(
        num_scalar_prefetch=0,
        grid=(b, nt),
        in_specs=[pl.BlockSpec((1, 1, 2 * tm), lambda bi, i: (bi * nt + i, 0, 0), memory_space=pltpu.SMEM),
                  pl.BlockSpec((1, 1, 2 * tm), lambda bi, i: (jnp.minimum(bi * nt + i + 1, last), 0, 0),
                               memory_space=pltpu.SMEM),
                  pl.BlockSpec((1, tm, D_MODEL), lambda bi, i: (bi, i, 0)),
                  pl.BlockSpec((1, tm, LANES), lambda bi, i: (bi, i, 0)),
                  pl.BlockSpec(memory_space=pl.ANY),
                  _const_spec((1, D_MODEL)),
                  pl.BlockSpec((1, 1, D_MODEL), lambda bi, i: (bi, 0, 0))],
        out_specs=pl.BlockSpec((1, tm, D_MODEL), lambda bi, i: (bi, i, 0)),
        scratch_shapes=[pltpu.VMEM((2, 2, tm, D_MODEL), F32), pltpu.SemaphoreType.DMA((2,))],
    )
    dest3 = _dest_blocks(dest, tm)
    return pl.pallas_call(
        functools.partial(_final_kernel, tm=tm),
        out_shape=jax.ShapeDtypeStruct((b, l, D_MODEL), F32),
        grid_spec=grid_spec,
        compiler_params=_params("arbitrary", "arbitrary"),
        name="final",
    )(dest3, dest3, x1, route, ys, g_post.reshape(1, D_MODEL), g2)


def _rot_half(w):
    f = ROPE_FREQS
    parts = []
    for seg in range(QK_ROPE // (2 * f)):
        x0 = w[..., 2 * f * seg:2 * f * seg + f]
        x1 = w[..., 2 * f * seg + f:2 * f * (seg + 1)]
        parts += [-x1, x0]
    return jnp.concatenate(parts, axis=-1)


def _rope_tables(seq):
    rows = seq // GRID_W
    row = jnp.repeat(jnp.arange(rows, dtype=F32), GRID_W)
    col = jnp.tile(jnp.arange(GRID_W, dtype=F32), rows)
    freqs = 1.0 / (ROPE_BASE ** (jnp.arange(ROPE_FREQS, dtype=F32) / ROPE_FREQS))
    ang_r = row[:, None] * freqs[None, :]
    ang_c = col[:, None] * freqs[None, :]
    ang = jnp.concatenate([ang_r, ang_r, ang_c, ang_c], axis=-1)
    cos, sin = jnp.cos(ang), jnp.sin(ang)
    tabk = jnp.concatenate([cos, sin], axis=-1)
    qs = ATTN_SCALE * LOG2_E
    tabq = jnp.concatenate([jnp.full((seq, QK_NOPE), qs, F32), tabk * qs], axis=-1)
    return tabq, tabk


def _tile(n, pref):
    return pref if n % pref == 0 else n


def kernel(x, c, ctx, c_ctx, w_mod, b_mod, norm_pre_mix, norm_post_mix, norm_pre_ffn, norm_post_ffn, w_in, pool_w,
           pool_scale, w_pool_up, q_norm, w_uq, kv_norm, w_ukv, w_attn_o, w_out, router_group_w, router_group_b,
           router_expert_w, router_expert_b, w_gate, w_up, w_down):
    bsz, seq, _ = x.shape
    assert w_mod.shape[0] == 1, "single-layer block"

    rows = -(-(bsz + 1) // 8) * 8
    cvec = jnp.concatenate([c, c_ctx[None, :], jnp.zeros((rows - bsz - 1, D_MODEL), F32)], axis=0)
    mod = _modulation(cvec, w_mod[0], b_mod[0])
    mod_x = mod[:bsz].reshape(bsz, N_MOD, 1, D_MODEL)
    sh1, sc1, g1, sh2, sc2, g2 = (mod_x[:, k] for k in range(N_MOD))
    mod_c = mod[bsz].reshape(N_MOD, 1, 1, D_MODEL)
    sh1c, sc1c = mod_c[0], mod_c[1]

    wi = w_in[0].astype(BF16)
    s0, s1, s2, s3 = POOL_DIM, POOL_DIM + Q_LORA, POOL_DIM + Q_LORA + KV_LORA, POOL_DIM + Q_LORA + KV_LORA + QK_ROPE
    w_pool, w_qc, w_kvc, w_kr, w_gates = wi[:, :s0], wi[:, s0:s1], wi[:, s1:s2], wi[:, s2:s3], wi[:, s3:]
    w1 = jnp.concatenate([w_gates, w_pool, w_qc, w_kvc, w_kr, _rot_half(w_kr),
                          jnp.zeros((D_MODEL, IN_COLS_PAD - IN_COLS), BF16)], axis=1)
    w1c = jnp.concatenate([w_kvc, w_kr, w_kr], axis=1)
    p1 = _in_proj(x, norm_pre_mix[0], sh1, sc1, w1, _tile(seq, 1024), IN_BLOCK_N)
    pc = _in_proj(ctx, norm_pre_mix[0], sh1c, sc1c, w1c, _tile(ctx.shape[1], 1024), w1c.shape[1])

    wq = w_uq[0].astype(BF16).reshape(Q_LORA, MLA_HEADS, QK_HEAD)
    wq_ext = jnp.concatenate([wq, _rot_half(wq[:, :, QK_NOPE:])], axis=-1).reshape(Q_LORA, MLA_HEADS * HEAD_PAD)
    wkv = w_ukv[0].astype(BF16).reshape(KV_LORA, MLA_HEADS, QK_NOPE + V_HEAD)
    wkv_re = jnp.concatenate([wkv[:, :, :QK_NOPE].reshape(KV_LORA, -1), wkv[:, :, QK_NOPE:].reshape(KV_LORA, -1)],
                             axis=1)
    tabq, tabk = _rope_tables(seq)
    q, k_lat, v_lat = _mla_latent(p1, q_norm[0], kv_norm[0], wq_ext, wkv_re, tabq, tabk, _tile(seq, 512))
    k_ctx, v_ctx = _mla_ctx(pc, kv_norm[0], wkv_re, _tile(ctx.shape[1], 256))

    attn = _attention(q, k_lat, v_lat, k_ctx, v_ctx, _tile(seq, ATTN_TQ), _tile(seq, ATTN_CHUNK), ATTN_HEADS)

    merged = _merge(p1, attn, pool_w[0].astype(BF16), pool_scale[0], w_pool_up[0].astype(BF16),
                    w_attn_o[0].astype(BF16), _tile(seq, 512))

    wr = jnp.concatenate([router_expert_w[0], router_group_w[0],
                          jnp.zeros((D_MODEL, LANES - N_EXPERTS - N_GROUPS), F32)], axis=1)
    wr_hi = wr.astype(BF16)
    wr_lo = jnp.concatenate([wr_hi, (wr - wr_hi.astype(F32)).astype(BF16)], axis=1)
    br = jnp.concatenate([router_expert_b[0], router_group_b[0],
                          jnp.zeros((LANES - N_EXPERTS - N_GROUPS,), F32)]).reshape(1, LANES)
    x1, hw, route, counts = _out_route(merged, x, w_out[0].astype(BF16), norm_post_mix[0], g1, norm_pre_ffn[0],
                                       sh2, sc2, wr_hi, wr_lo, br, _tile(seq, 512))

    tokens = bsz * seq
    n_tiles_max = 2 * tokens // MOE_TILE + MOE_TILES_MAX_EXTRA
    cnt = counts[0, :N_EXPERTS].astype(jnp.int32)
    tiles_per = (cnt + MOE_TILE - 1) // MOE_TILE
    tile_end = jnp.cumsum(tiles_per)
    seg_start = (tile_end - tiles_per) * MOE_TILE
    n_tiles = tile_end[-1:]
    rt = route.reshape(tokens, LANES)[:, :8].T.astype(jnp.int32)
    experts = jnp.arange(N_EXPERTS, dtype=jnp.int32)[:, None]

    def dest_of(e_lane, r_lane):
        seg = jnp.sum(jnp.where(rt[e_lane][None, :] == experts, seg_start[:, None], 0), axis=0)
        return rt[r_lane] + seg

    dest = (dest_of(ROUTE_E1, ROUTE_R1), dest_of(ROUTE_E2, ROUTE_R2))

    xs = _dispatch(hw.reshape(tokens, D_MODEL), dest, seg_start + cnt, tiles_per * MOE_TILE - cnt,
                   n_tiles, n_tiles_max, _tile(tokens, 1024))
    ys = _moe(xs, tile_end - tiles_per, tiles_per, n_tiles, w_gate[0], w_up[0], w_down[0])
    return _final(x1, route, ys, dest, norm_post_ffn[0], g2, _tile(seq, 256))


def _dest_blocks(dest, tm):
    d1, d2 = dest
    n = d1.shape[0] // tm
    return jnp.concatenate([d1.reshape(n, 1, tm), d2.reshape(n, 1, tm)], axis=2)
```

```python
import functools

import jax
import jax.numpy as jnp
from jax import lax
from jax.experimental import pallas as pl
from jax.experimental.pallas import tpu as pltpu

F32 = jnp.float32
BF16 = jnp.bfloat16

D_MODEL = 2048
N_MOD = 6
EPS = 1e-6
GRID_W = 64
POOL_WINDOWS = (2, 4, 8, 16)
POOL_GROUPS = 4
POOL_GROUP_DIM = D_MODEL // 8
POOL_DIM = POOL_GROUPS * POOL_GROUP_DIM
MLA_HEADS = D_MODEL // 128
Q_LORA = D_MODEL // 4
KV_LORA = D_MODEL // 4
QK_NOPE = 128
QK_ROPE = 64
V_HEAD = 128
QK_HEAD = QK_NOPE + QK_ROPE
ROPE_FREQS = QK_ROPE // 4
ROPE_BASE = 10000.0
ATTN_SCALE = QK_HEAD ** -0.5
LOG2_E = 1.4426950408889634
N_GROUPS = 4
EXPERTS_PER_GROUP = 8
N_EXPERTS = N_GROUPS * EXPERTS_PER_GROUP
D_EXPERT = D_MODEL // 4

LANES = 128
HEAD_PAD = 2 * LANES
HALO = 16
ATTN_TQ = 1024
ATTN_CHUNK = 1024
ATTN_HEADS = 2
V7X_VMEM_BYTES = 64 * 1024 * 1024
VMEM_LIMIT = 56 * 1024 * 1024

COL_GATE_POOL = 0
COL_GATE_ATTN = D_MODEL
COL_POOL = 2 * D_MODEL
COL_QC = COL_POOL + POOL_DIM
COL_KVC = COL_QC + Q_LORA
COL_KRR = COL_KVC + KV_LORA
IN_COLS = COL_KRR + LANES
IN_COLS_PAD = 6400
IN_BLOCK_N = 1280


def _params(*sem):
    return pltpu.CompilerParams(dimension_semantics=sem, vmem_limit_bytes=VMEM_LIMIT)


def _sigmoid(v):
    return 1.0 / (1.0 + jnp.exp(-v))


def _rms(v, g):
    v = v.astype(F32)
    return v * lax.rsqrt(jnp.mean(v * v, axis=-1, keepdims=True) + EPS) * g


def _const_spec(shape):
    nd = len(shape)
    return pl.BlockSpec(shape, lambda *_: (0,) * nd, pipeline_mode=pl.Buffered(1))


def _mod_kernel(c_ref, w_ref, b_ref, o_ref):
    a = c_ref[...]
    a = a * _sigmoid(a)
    o_ref[...] = jnp.dot(a.astype(BF16), w_ref[...].astype(BF16), preferred_element_type=F32) + b_ref[...]


def _modulation(cvec, w_mod, b_mod):
    rows = cvec.shape[0]
    n = w_mod.shape[1]
    bn = 1536
    return pl.pallas_call(
        _mod_kernel,
        out_shape=jax.ShapeDtypeStruct((rows, n), F32),
        grid=(n // bn,),
        in_specs=[pl.BlockSpec((rows, D_MODEL), lambda j: (0, 0)),
                  pl.BlockSpec((D_MODEL, bn), lambda j: (0, j)),
                  pl.BlockSpec((1, bn), lambda j: (0, j))],
        out_specs=pl.BlockSpec((rows, bn), lambda j: (0, j)),
        compiler_params=_params("arbitrary"),
        name="mod",
    )(cvec, w_mod, b_mod.reshape(1, n))


W_IN_GATES = POOL_DIM + Q_LORA + KV_LORA + QK_ROPE
W_IN_COLS = W_IN_GATES + 2 * D_MODEL


def _regroup_kernel(w_ref, tail_ref, o_ref):
    w = w_ref[...]
    o_ref[:, :COL_POOL] = w[:, W_IN_GATES:].astype(BF16)
    o_ref[:, COL_POOL:COL_KRR] = w[:, :COL_KRR - COL_POOL].astype(BF16)
    o_ref[:, COL_KRR:] = tail_ref[...]


def _regroup_w_in(w, tail):
    rows = 256
    return pl.pallas_call(
        _regroup_kernel,
        out_shape=jax.ShapeDtypeStruct((D_MODEL, IN_COLS_PAD), BF16),
        grid=(D_MODEL // rows,),
        in_specs=[pl.BlockSpec((rows, W_IN_COLS), lambda i: (i, 0)),
                  pl.BlockSpec((rows, IN_COLS_PAD - COL_KRR), lambda i: (i, 0))],
        out_specs=pl.BlockSpec((rows, IN_COLS_PAD), lambda i: (i, 0)),
        compiler_params=_params("arbitrary"),
        name="regroup_w_in",
    )(w, tail)

def _inproj_kernel(x_ref, g_ref, sh_ref, sc_ref, w_ref, o_ref, h_ref):
    @pl.when(pl.program_id(2) == 0)
    def _():
        h = _rms(x_ref[0], g_ref[...]) * (1.0 + sc_ref[0]) + sh_ref[0]
        h_ref[...] = h.astype(BF16)

    o_ref[0] = jnp.dot(h_ref[...], w_ref[...], preferred_element_type=F32).astype(o_ref.dtype)


def _in_proj(x, gain, shift, scale, w, tm, bn):
    b, l, _ = x.shape
    n = w.shape[1]
    per_batch = shift.shape[0] == b
    mod_map = (lambda bi, i, j: (bi, 0, 0)) if per_batch else (lambda bi, i, j: (0, 0, 0))
    return pl.pallas_call(
        _inproj_kernel,
        out_shape=jax.ShapeDtypeStruct((b, l, n), BF16),
        grid=(b, l // tm, n // bn),
        in_specs=[pl.BlockSpec((1, tm, D_MODEL), lambda bi, i, j: (bi, i, 0)),
                  pl.BlockSpec((1, D_MODEL), lambda bi, i, j: (0, 0)),
                  pl.BlockSpec((1, 1, D_MODEL), mod_map),
                  pl.BlockSpec((1, 1, D_MODEL), mod_map),
                  pl.BlockSpec((D_MODEL, bn), lambda bi, i, j: (0, j))],
        out_specs=pl.BlockSpec((1, tm, bn), lambda bi, i, j: (bi, i, j)),
        scratch_shapes=[pltpu.VMEM((tm, D_MODEL), BF16)],
        compiler_params=_params("arbitrary", "arbitrary", "arbitrary"),
        name="in_proj",
    )(x, gain.reshape(1, D_MODEL), shift, scale, w)


def _mla_latent_kernel(qc_ref, kvc_ref, krr_ref, qn_ref, kvn_ref, wq_ref, wkv_ref, tq_ref, tk_ref,
                       q_out, k_out, v_out):
    qn = _rms(qc_ref[0], qn_ref[...]).astype(BF16)
    q = jnp.dot(qn, wq_ref[...], preferred_element_type=F32)
    tq = tq_ref[...]
    for h in range(MLA_HEADS):
        q_out[0, :, h * HEAD_PAD:(h + 1) * HEAD_PAD] = (q[:, h * HEAD_PAD:(h + 1) * HEAD_PAD] * tq).astype(BF16)
    kvn = _rms(kvc_ref[0], kvn_ref[...]).astype(BF16)
    kv = jnp.dot(kvn, wkv_ref[...], preferred_element_type=F32)
    u = krr_ref[0].astype(F32) * tk_ref[...]
    krot = (u + pltpu.roll(u, QK_ROPE, axis=1)).astype(BF16)
    _store_kv(kv, krot, k_out, v_out)


def _store_kv(kv, k_rope, k_out, v_out):
    tm = kv.shape[0]
    lane = lax.broadcasted_iota(jnp.int32, (tm, LANES), 1)
    ones_col = jnp.where(lane == 0, 1.0, 0.0).astype(BF16)
    v0 = MLA_HEADS * QK_NOPE
    for h in range(MLA_HEADS):
        k_out[0, :, h * HEAD_PAD:h * HEAD_PAD + QK_NOPE] = kv[:, h * QK_NOPE:(h + 1) * QK_NOPE].astype(BF16)
        k_out[0, :, h * HEAD_PAD + QK_NOPE:(h + 1) * HEAD_PAD] = k_rope
        v_out[0, :, h * HEAD_PAD:h * HEAD_PAD + V_HEAD] = kv[:, v0 + h * V_HEAD:v0 + (h + 1) * V_HEAD].astype(BF16)
        v_out[0, :, h * HEAD_PAD + V_HEAD:(h + 1) * HEAD_PAD] = ones_col


def _mla_ctx_kernel(kvc_ref, krr_ref, kvn_ref, wkv_ref, k_out, v_out):
    kvn = _rms(kvc_ref[0], kvn_ref[...]).astype(BF16)
    kv = jnp.dot(kvn, wkv_ref[...], preferred_element_type=F32)
    _store_kv(kv, krr_ref[0], k_out, v_out)


def _mla_latent(p1, q_norm, kv_norm, wq, wkv, tabq, tabk, tm):
    b, l, _ = p1.shape
    hq = MLA_HEADS * HEAD_PAD
    hv = MLA_HEADS * HEAD_PAD
    return pl.pallas_call(
        _mla_latent_kernel,
        out_shape=(jax.ShapeDtypeStruct((b, l, hq), BF16),
                   jax.ShapeDtypeStruct((b, l, hq), BF16),
                   jax.ShapeDtypeStruct((b, l, hv), BF16)),
        grid=(b, l // tm),
        in_specs=[pl.BlockSpec((1, tm, Q_LORA), lambda bi, i: (bi, i, COL_QC // Q_LORA)),
                  pl.BlockSpec((1, tm, KV_LORA), lambda bi, i: (bi, i, COL_KVC // KV_LORA)),
                  pl.BlockSpec((1, tm, LANES), lambda bi, i: (bi, i, COL_KRR // LANES)),
                  _const_spec((1, Q_LORA)),
                  _const_spec((1, KV_LORA)),
                  _const_spec((Q_LORA, hq)),
                  _const_spec((KV_LORA, hq)),
                  pl.BlockSpec((tm, HEAD_PAD), lambda bi, i: (i, 0)),
                  pl.BlockSpec((tm, LANES), lambda bi, i: (i, 0))],
        out_specs=(pl.BlockSpec((1, tm, hq), lambda bi, i: (bi, i, 0)),
                   pl.BlockSpec((1, tm, hq), lambda bi, i: (bi, i, 0)),
                   pl.BlockSpec((1, tm, hv), lambda bi, i: (bi, i, 0))),
        compiler_params=_params("arbitrary", "arbitrary"),
        name="mla_latent",
    )(p1, p1, p1, q_norm.reshape(1, Q_LORA), kv_norm.reshape(1, KV_LORA), wq, wkv, tabq, tabk)


def _mla_ctx(pc, kv_norm, wkv, tm):
    b, c, _ = pc.shape
    hq = MLA_HEADS * HEAD_PAD
    hv = MLA_HEADS * HEAD_PAD
    return pl.pallas_call(
        _mla_ctx_kernel,
        out_shape=(jax.ShapeDtypeStruct((b, c, hq), BF16),
                   jax.ShapeDtypeStruct((b, c, hv), BF16)),
        grid=(b, c // tm),
        in_specs=[pl.BlockSpec((1, tm, KV_LORA), lambda bi, i: (bi, i, 0)),
                  pl.BlockSpec((1, tm, LANES), lambda bi, i: (bi, i, KV_LORA // LANES)),
                  _const_spec((1, KV_LORA)),
                  _const_spec((KV_LORA, hq))],
        out_specs=(pl.BlockSpec((1, tm, hq), lambda bi, i: (bi, i, 0)),
                   pl.BlockSpec((1, tm, hv), lambda bi, i: (bi, i, 0))),
        compiler_params=_params("arbitrary", "arbitrary"),
        name="mla_ctx",
    )(pc, pc, kv_norm.reshape(1, KV_LORA), wkv)


def _attn_kernel(q_ref, kl_ref, vl_ref, kc_ref, vc_ref, o_ref, *, chunk, heads):
    nt = (((1,), (1,)), ((), ()))
    seq = kl_ref.shape[1]
    segments = [(kl_ref, vl_ref, s0, chunk) for s0 in range(0, seq, chunk)] + [(kc_ref, vc_ref, 0, kc_ref.shape[1])]
    m = [None] * heads
    acc = [None] * heads
    for k_ref, v_ref, s0, size in segments:
        for h in range(heads):
            cols = slice(h * HEAD_PAD, (h + 1) * HEAD_PAD)
            s = lax.dot_general(q_ref[0, :, cols], k_ref[0, s0:s0 + size, cols], nt,
                                preferred_element_type=F32)
            m_seg = jnp.max(s, axis=-1, keepdims=True)
            m_new = m_seg if m[h] is None else jnp.maximum(m[h], m_seg)
            p = jnp.exp2(s - m_new).astype(BF16)
            pv = jnp.dot(p, v_ref[0, s0:s0 + size, cols], preferred_element_type=F32)
            acc[h] = pv if m[h] is None else acc[h] * jnp.exp2(m[h] - m_new) + pv
            m[h] = m_new
    for h in range(heads):
        o_ref[0, :, h * V_HEAD:(h + 1) * V_HEAD] = (acc[h][:, :V_HEAD] / acc[h][:, V_HEAD:V_HEAD + 1]).astype(o_ref.dtype)


def _attention(q, k_lat, v_lat, k_ctx, v_ctx, tq, chunk, heads):
    b, l, _ = q.shape
    c = k_ctx.shape[1]
    wide = heads * HEAD_PAD
    return pl.pallas_call(
        functools.partial(_attn_kernel, chunk=chunk, heads=heads),
        out_shape=jax.ShapeDtypeStruct((b, l, MLA_HEADS * V_HEAD), BF16),
        grid=(b, MLA_HEADS // heads, l // tq),
        in_specs=[pl.BlockSpec((1, tq, wide), lambda bi, h, i: (bi, i, h)),
                  pl.BlockSpec((1, l, wide), lambda bi, h, i: (bi, 0, h)),
                  pl.BlockSpec((1, l, wide), lambda bi, h, i: (bi, 0, h)),
                  pl.BlockSpec((1, c, wide), lambda bi, h, i: (bi, 0, h)),
                  pl.BlockSpec((1, c, wide), lambda bi, h, i: (bi, 0, h))],
        out_specs=pl.BlockSpec((1, tq, heads * V_HEAD), lambda bi, h, i: (bi, i, h)),
        compiler_params=_params("arbitrary", "arbitrary", "arbitrary"),
        name="attention",
    )(q, k_lat, v_lat, k_ctx, v_ctx)


def _merge_kernel(gp_ref, ga_ref, pool_ref, hb_ref, ha_ref, attn_ref, pw_ref, ps_ref, wpu_ref, wao_ref, o_ref,
                  *, tm, seq):
    i = pl.program_id(1)
    hb = hb_ref[0]
    ha = ha_ref[0]
    hb = jnp.where(i == 0, jnp.zeros_like(hb), hb)
    ha = jnp.where(i == pl.num_programs(1) - 1, jnp.zeros_like(ha), ha)
    u = pool_ref[0]
    uext = jnp.concatenate([hb, u, ha], axis=0)
    rows = lax.broadcasted_iota(jnp.int32, (tm, tm + 2 * HALO), 0)
    cols = lax.broadcasted_iota(jnp.int32, (tm, tm + 2 * HALO), 1)
    off = cols - HALO - rows
    t = i * tm + lax.broadcasted_iota(jnp.int32, (tm, 1), 0)
    mixed = []
    for g, w in enumerate(POOL_WINDOWS):
        lo, hi = g * POOL_GROUP_DIM, (g + 1) * POOL_GROUP_DIM
        band = jnp.where((off >= -(w // 2)) & (off < w // 2), 1.0, 0.0).astype(BF16)
        wsum = jnp.dot(band, uext[:, lo:hi], preferred_element_type=F32)
        cnt = (jnp.minimum(t + w // 2, seq) - jnp.maximum(t - w // 2, 0)).astype(F32)
        pooled = wsum / cnt - u[:, lo:hi].astype(F32)
        mixed.append(jnp.dot(pooled.astype(BF16), pw_ref[g], preferred_element_type=F32))
    pool_out = jnp.concatenate(mixed, axis=-1) * ps_ref[...]
    pool_up = jnp.dot(pool_out.astype(BF16), wpu_ref[...], preferred_element_type=F32)
    attn_o = jnp.dot(attn_ref[0], wao_ref[...], preferred_element_type=F32)
    merged = _sigmoid(gp_ref[0].astype(F32)) * pool_up + _sigmoid(ga_ref[0].astype(F32)) * attn_o
    o_ref[0] = merged.astype(o_ref.dtype)


def _merge(p1, attn, pool_w, pool_scale, w_pool_up, w_attn_o, tm):
    b, l, _ = p1.shape
    hb = tm // HALO
    nhalo = l // HALO
    return pl.pallas_call(
        functools.partial(_merge_kernel, tm=tm, seq=l),
        out_shape=jax.ShapeDtypeStruct((b, l, D_MODEL), BF16),
        grid=(b, l // tm),
        in_specs=[pl.BlockSpec((1, tm, D_MODEL), lambda bi, i: (bi, i, COL_GATE_POOL // D_MODEL)),
                  pl.BlockSpec((1, tm, D_MODEL), lambda bi, i: (bi, i, COL_GATE_ATTN // D_MODEL)),
                  pl.BlockSpec((1, tm, POOL_DIM), lambda bi, i: (bi, i, COL_POOL // POOL_DIM)),
                  pl.BlockSpec((1, HALO, POOL_DIM),
                               lambda bi, i: (bi, jnp.maximum(i * hb - 1, 0), COL_POOL // POOL_DIM)),
                  pl.BlockSpec((1, HALO, POOL_DIM),
                               lambda bi, i: (bi, jnp.minimum((i + 1) * hb, nhalo - 1), COL_POOL // POOL_DIM)),
                  pl.BlockSpec((1, tm, D_MODEL), lambda bi, i: (bi, i, 0)),
                  _const_spec((POOL_GROUPS, POOL_GROUP_DIM, POOL_GROUP_DIM)),
                  _const_spec((1, POOL_DIM)),
                  _const_spec((POOL_DIM, D_MODEL)),
                  _const_spec((D_MODEL, D_MODEL))],
        out_specs=pl.BlockSpec((1, tm, D_MODEL), lambda bi, i: (bi, i, 0)),
        compiler_params=_params("arbitrary", "arbitrary"),
        name="merge",
    )(p1, p1, p1, p1, p1, attn, pool_w, pool_scale.reshape(1, POOL_DIM), w_pool_up, w_attn_o)


ROUTE_E1, ROUTE_E2, ROUTE_W1, ROUTE_W2, ROUTE_R1, ROUTE_R2 = 0, 1, 2, 3, 4, 5
GROUP_LANE0 = N_EXPERTS
MOE_TILE = 512
MOE_TILES_MAX_EXTRA = N_EXPERTS
DMA_UNROLL = 8

def _route(logits, carry):
    tm = logits.shape[0]
    lane = lax.broadcasted_iota(jnp.int32, (tm, LANES), 1)
    lanef = lane.astype(F32)
    neg = jnp.float32(-jnp.inf)
    big = jnp.float32(1e9)
    is_g = (lane >= GROUP_LANE0) & (lane < GROUP_LANE0 + N_GROUPS)
    lg = jnp.where(is_g, logits, neg)
    mg = jnp.max(lg, axis=-1, keepdims=True)
    pg_top = 1.0 / jnp.sum(jnp.exp(lg - mg), axis=-1, keepdims=True)
    g_idx = jnp.min(jnp.where(lg == mg, lanef, big), axis=-1, keepdims=True) - GROUP_LANE0
    e_lo = g_idx * EXPERTS_PER_GROUP
    in_grp = (lanef >= e_lo) & (lanef < e_lo + EXPERTS_PER_GROUP)
    ls = jnp.where(in_grp, logits, neg)
    v1 = jnp.max(ls, axis=-1, keepdims=True)
    i1 = jnp.min(jnp.where(ls == v1, lanef, big), axis=-1, keepdims=True)
    ls2 = jnp.where(lanef == i1, neg, ls)
    v2 = jnp.max(ls2, axis=-1, keepdims=True)
    i2 = jnp.min(jnp.where(ls2 == v2, lanef, big), axis=-1, keepdims=True)
    den = jnp.sum(jnp.exp(ls - v1), axis=-1, keepdims=True)
    pe1 = 1.0 / den
    pe2 = jnp.exp(v2 - v1) / den
    w1 = pg_top * pe1 / (pe1 + pe2)
    w2 = pg_top * pe2 / (pe1 + pe2)
    sel = jnp.where((lanef == i1) | (lanef == i2), 1.0, 0.0)
    r = lax.broadcasted_iota(jnp.int32, (tm, tm), 0)
    cc = lax.broadcasted_iota(jnp.int32, (tm, tm), 1)
    before = jnp.where(cc < r, 1.0, 0.0).astype(BF16)
    seen = jnp.dot(before, sel.astype(BF16), preferred_element_type=F32) + carry
    r1 = jnp.sum(jnp.where(lanef == i1, seen, 0.0), axis=-1, keepdims=True)
    r2 = jnp.sum(jnp.where(lanef == i2, seen, 0.0), axis=-1, keepdims=True)
    rec = jnp.zeros((tm, LANES), F32)
    for pos, val in ((ROUTE_E1, i1), (ROUTE_E2, i2), (ROUTE_W1, w1), (ROUTE_W2, w2), (ROUTE_R1, r1), (ROUTE_R2, r2)):
        rec = jnp.where(lane == pos, val, rec)
    return rec, carry + jnp.sum(sel, axis=0, keepdims=True)


def _out_route_kernel(m_ref, x_ref, wo_ref, gpost_ref, g1_ref, gpre_ref, sh_ref, sc_ref, wrh_ref, wrl_ref, br_ref,
                      x1_ref, hw_ref, route_ref, cnt_ref, carry_ref):
    @pl.when((pl.program_id(0) == 0) & (pl.program_id(1) == 0))
    def _():
        carry_ref[...] = jnp.zeros_like(carry_ref)

    y = jnp.dot(m_ref[0], wo_ref[...], preferred_element_type=F32)
    x1 = x_ref[0] + g1_ref[0] * _rms(y, gpost_ref[...])
    x1_ref[0] = x1
    h2 = _rms(x1, gpre_ref[...]) * (1.0 + sc_ref[0]) + sh_ref[0]
    hi = h2.astype(BF16)
    lo = (h2 - hi.astype(F32)).astype(BF16)
    hw_ref[0] = h2
    both = jnp.dot(hi, wrl_ref[...], preferred_element_type=F32)
    logits = (both[:, :LANES] + both[:, LANES:]
              + jnp.dot(lo, wrh_ref[...], preferred_element_type=F32)) + br_ref[...]
    rec, carry = _route(logits, carry_ref[...])
    route_ref[0] = rec
    carry_ref[...] = carry
    cnt_ref[...] = carry


def _out_route(merged, x, w_out, g_post, g1, g_pre, sh2, sc2, wr_hi, wr_lo, br, tm):
    b, l, _ = x.shape
    tile = lambda w: pl.BlockSpec((1, tm, w), lambda bi, i: (bi, i, 0))
    per_b = pl.BlockSpec((1, 1, D_MODEL), lambda bi, i: (bi, 0, 0))
    return pl.pallas_call(
        _out_route_kernel,
        out_shape=(jax.ShapeDtypeStruct((b, l, D_MODEL), F32),
                   jax.ShapeDtypeStruct((b, l, D_MODEL), F32),
                   jax.ShapeDtypeStruct((b, l, LANES), F32),
                   jax.ShapeDtypeStruct((1, LANES), F32)),
        grid=(b, l // tm),
        in_specs=[tile(D_MODEL), tile(D_MODEL),
                  _const_spec((D_MODEL, D_MODEL)),
                  _const_spec((1, D_MODEL)),
                  per_b,
                  _const_spec((1, D_MODEL)),
                  per_b, per_b,
                  _const_spec((D_MODEL, LANES)),
                  _const_spec((D_MODEL, 2 * LANES)),
                  _const_spec((1, LANES))],
        out_specs=(tile(D_MODEL), tile(D_MODEL), tile(LANES), pl.BlockSpec((1, LANES), lambda bi, i: (0, 0))),
        scratch_shapes=[pltpu.VMEM((1, LANES), F32)],
        compiler_params=_params("arbitrary", "arbitrary"),
        name="out_route",
    )(merged, x, w_out, g_post.reshape(1, D_MODEL), g1, g_pre.reshape(1, D_MODEL), sh2, sc2, wr_hi, wr_lo, br)


def _row_copy(src_ref, src_row, dst_ref, dst_row, sem):
    return pltpu.make_async_copy(src_ref.at[pl.ds(src_row, 1)], dst_ref.at[pl.ds(dst_row, 1)], sem)


def _dispatch_kernel(pad_start_ref, pad_len_ref, n_tiles_ref, dest_ref, h_ref, xs_ref, zero_ref, sem, *, tm, n_max):
    i = pl.program_id(0)

    def tile_copy(j):
        return pltpu.make_async_copy(zero_ref, xs_ref.at[pl.ds(j * MOE_TILE, MOE_TILE)], sem)

    def start_token(r, carry):
        for k in range(2):
            _row_copy(h_ref, r, xs_ref, dest_ref[0, 0, k * tm + r], sem).start(priority=k)
        return carry

    def wait_token(r, carry):
        for k in range(2):
            _row_copy(h_ref, 0, xs_ref, 0, sem).wait()
        return carry

    lax.fori_loop(0, tm, start_token, 0, unroll=DMA_UNROLL)

    @pl.when(i == 0)
    def _():
        zero_ref[...] = jnp.zeros_like(zero_ref)

        def start_expert(e, carry):
            def start_row(j, c):
                _row_copy(zero_ref, 0, xs_ref, pad_start_ref[e] + j, sem).start()
                return c
            return lax.fori_loop(0, pad_len_ref[e], start_row, carry)

        def wait_expert(e, carry):
            def wait_row(j, c):
                _row_copy(zero_ref, 0, xs_ref, 0, sem).wait()
                return c
            return lax.fori_loop(0, pad_len_ref[e], wait_row, carry)

        lax.fori_loop(0, N_EXPERTS, start_expert, 0)
        lax.fori_loop(n_tiles_ref[0], n_max, lambda j, c: (tile_copy(j).start(), c)[1], 0)
        lax.fori_loop(0, N_EXPERTS, wait_expert, 0)
        lax.fori_loop(n_tiles_ref[0], n_max, lambda j, c: (tile_copy(j).wait(), c)[1], 0)

    lax.fori_loop(0, tm, wait_token, 0, unroll=DMA_UNROLL)


def _dispatch(hw, dest, pad_start, pad_len, n_tiles, n_max, tm):
    t = hw.shape[0]
    grid_spec = pltpu.PrefetchScalarGridSpec(
        num_scalar_prefetch=3,
        grid=(t // tm,),
        in_specs=[pl.BlockSpec((1, 1, 2 * tm), lambda i, *_: (i, 0, 0), memory_space=pltpu.SMEM),
                  pl.BlockSpec((tm, D_MODEL), lambda i, *_: (i, 0))],
        out_specs=pl.BlockSpec(memory_space=pl.ANY),
        scratch_shapes=[pltpu.VMEM((MOE_TILE, D_MODEL), F32), pltpu.SemaphoreType.DMA(())],
    )
    return pl.pallas_call(
        functools.partial(_dispatch_kernel, tm=tm, n_max=n_max),
        out_shape=jax.ShapeDtypeStruct((n_max * MOE_TILE, D_MODEL), F32),
        grid_spec=grid_spec,
        compiler_params=_params("arbitrary"),
        name="dispatch",
    )(pad_start, pad_len, n_tiles, _dest_blocks(dest, tm), hw)


def _moe_kernel(tile_expert_ref, tile_first_ref, n_tiles_ref, x_ref, wg_ref, wu_ref, wd_ref, y_ref,
                wg_bf, wu_bf, wd_bf):
    j = pl.program_id(0)

    @pl.when(j < n_tiles_ref[0])
    def _():
        @pl.when(tile_first_ref[j] == 1)
        def _():
            wg_bf[...] = wg_ref[0].astype(BF16)
            wu_bf[...] = wu_ref[0].astype(BF16)
            wd_bf[...] = wd_ref[0].astype(BF16)

        h = x_ref[...].astype(BF16)
        gate = jnp.dot(h, wg_bf[...], preferred_element_type=F32)
        up = jnp.dot(h, wu_bf[...], preferred_element_type=F32)
        act = (gate * _sigmoid(gate)) * up
        y_ref[...] = jnp.dot(act.astype(BF16), wd_bf[...], preferred_element_type=F32)

    @pl.when(j >= n_tiles_ref[0])
    def _():
        y_ref[...] = jnp.zeros_like(y_ref)


def _moe(xs, tile_expert, tile_first, n_tiles, w_gate, w_up, w_down):
    n_max = xs.shape[0] // MOE_TILE
    row_map = lambda j, te, tf, nt: (jnp.minimum(j, nt[0] - 1), 0)
    out_map = lambda j, te, tf, nt: (j, 0)
    w_map = lambda j, te, tf, nt: (te[j], 0, 0)
    grid_spec = pltpu.PrefetchScalarGridSpec(
        num_scalar_prefetch=3,
        grid=(n_max,),
        in_specs=[pl.BlockSpec((MOE_TILE, D_MODEL), row_map),
                  pl.BlockSpec((1, D_MODEL, D_EXPERT), w_map),
                  pl.BlockSpec((1, D_MODEL, D_EXPERT), w_map),
                  pl.BlockSpec((1, D_EXPERT, D_MODEL), w_map)],
        out_specs=pl.BlockSpec((MOE_TILE, D_MODEL), out_map),
        scratch_shapes=[pltpu.VMEM((D_MODEL, D_EXPERT), BF16), pltpu.VMEM((D_MODEL, D_EXPERT), BF16),
                        pltpu.VMEM((D_EXPERT, D_MODEL), BF16)],
    )
    return pl.pallas_call(
        _moe_kernel,
        out_shape=jax.ShapeDtypeStruct((xs.shape[0], D_MODEL), F32),
        grid_spec=grid_spec,
        compiler_params=_params("arbitrary"),
        name="moe",
    )(tile_expert, tile_first, n_tiles, xs, w_gate, w_up, w_down)


def _final_kernel(dest_ref, dest_next_ref, x_ref, route_ref, ys_ref, g_ref, g2_ref, o_ref, rows_ref, sems, *, tm):
    step = pl.program_id(0) * pl.num_programs(1) + pl.program_id(1)
    n_steps = pl.num_programs(0) * pl.num_programs(1)
    slot = step % 2

    def start_rows(d_ref, s):
        def body(r, carry):
            for k in range(2):
                _row_copy(ys_ref, d_ref[0, 0, k * tm + r], rows_ref.at[s, k], r, sems.at[s]).start(priority=k)
            return carry
        lax.fori_loop(0, tm, body, 0, unroll=DMA_UNROLL)

    def wait_rows(s):
        def body(r, carry):
            for k in range(2):
                _row_copy(ys_ref, 0, rows_ref.at[s, k], 0, sems.at[s]).wait()
            return carry
        lax.fori_loop(0, tm, body, 0, unroll=DMA_UNROLL)

    @pl.when(step == 0)
    def _():
        start_rows(dest_ref, 0)

    @pl.when(step + 1 < n_steps)
    def _():
        start_rows(dest_next_ref, 1 - slot)

    wait_rows(slot)
    route = route_ref[0]
    w1 = route[:, ROUTE_W1:ROUTE_W1 + 1]
    w2 = route[:, ROUTE_W2:ROUTE_W2 + 1]
    y = w1 * rows_ref[slot, 0] + w2 * rows_ref[slot, 1]
    o_ref[0] = x_ref[0] + g2_ref[0] * _rms(y, g_ref[...])


def _final(x1, route, ys, dest, g_post, g2, tm):
    b, l, _ = x1.shape
    nt = l // tm
    last = b * nt - 1
    grid_spec = pltpu.PrefetchScalarGridSpec(
        num_scalar_prefetch=0,
        grid=(b, nt),
        in_specs=[pl.BlockSpec((1, 1, 2 * tm), lambda bi, i: (bi * nt + i, 0, 0), memory_space=pltpu.SMEM),
                  pl.BlockSpec((1, 1, 2 * tm), lambda bi, i: (jnp.minimum(bi * nt + i + 1, last), 0, 0),
                               memory_space=pltpu.SMEM),
                  pl.BlockSpec((1, tm, D_MODEL), lambda bi, i: (bi, i, 0)),
                  pl.BlockSpec((1, tm, LANES), lambda bi, i: (bi, i, 0)),
                  pl.BlockSpec(memory_space=pl.ANY),
                  _const_spec((1, D_MODEL)),
                  pl.BlockSpec((1, 1, D_MODEL), lambda bi, i: (bi, 0, 0))],
        out_specs=pl.BlockSpec((1, tm, D_MODEL), lambda bi, i: (bi, i, 0)),
        scratch_shapes=[pltpu.VMEM((2, 2, tm, D_MODEL), F32), pltpu.SemaphoreType.DMA((2,))],
    )
    dest3 = _dest_blocks(dest, tm)
    return pl.pallas_call(
        functools.partial(_final_kernel, tm=tm),
        out_shape=jax.ShapeDtypeStruct((b, l, D_MODEL), F32),
        grid_spec=grid_spec,
        compiler_params=_params("arbitrary", "arbitrary"),
        name="final",
    )(dest3, dest3, x1, route, ys, g_post.reshape(1, D_MODEL), g2)


def _rot_half(w):
    f = ROPE_FREQS
    parts = []
    for seg in range(QK_ROPE // (2 * f)):
        x0 = w[..., 2 * f * seg:2 * f * seg + f]
        x1 = w[..., 2 * f * seg + f:2 * f * (seg + 1)]
        parts += [-x1, x0]
    return jnp.concatenate(parts, axis=-1)


def _rope_tables(seq):
    rows = seq // GRID_W
    row = jnp.repeat(jnp.arange(rows, dtype=F32), GRID_W)
    col = jnp.tile(jnp.arange(GRID_W, dtype=F32), rows)
    freqs = 1.0 / (ROPE_BASE ** (jnp.arange(ROPE_FREQS, dtype=F32) / ROPE_FREQS))
    ang_r = row[:, None] * freqs[None, :]
    ang_c = col[:, None] * freqs[None, :]
    ang = jnp.concatenate([ang_r, ang_r, ang_c, ang_c], axis=-1)
    cos, sin = jnp.cos(ang), jnp.sin(ang)
    tabk = jnp.concatenate([cos, sin], axis=-1)
    qs = ATTN_SCALE * LOG2_E
    tabq = jnp.concatenate([jnp.full((seq, QK_NOPE), qs, F32), tabk * qs], axis=-1)
    return tabq, tabk


def _tile(n, pref):
    return pref if n % pref == 0 else n


def kernel(x, c, ctx, c_ctx, w_mod, b_mod, norm_pre_mix, norm_post_mix, norm_pre_ffn, norm_post_ffn, w_in, pool_w,
           pool_scale, w_pool_up, q_norm, w_uq, kv_norm, w_ukv, w_attn_o, w_out, router_group_w, router_group_b,
           router_expert_w, router_expert_b, w_gate, w_up, w_down):
    bsz, seq, _ = x.shape
    assert w_mod.shape[0] == 1, "single-layer block"

    rows = -(-(bsz + 1) // 8) * 8
    cvec = jnp.concatenate([c, c_ctx[None, :], jnp.zeros((rows - bsz - 1, D_MODEL), F32)], axis=0)
    mod = _modulation(cvec, w_mod[0], b_mod[0])
    mod_x = mod[:bsz].reshape(bsz, N_MOD, 1, D_MODEL)
    sh1, sc1, g1, sh2, sc2, g2 = (mod_x[:, k] for k in range(N_MOD))
    mod_c = mod[bsz].reshape(N_MOD, 1, 1, D_MODEL)
    sh1c, sc1c = mod_c[0], mod_c[1]

    s1, s2, s3 = POOL_DIM + Q_LORA, POOL_DIM + Q_LORA + KV_LORA, POOL_DIM + Q_LORA + KV_LORA + QK_ROPE
    w_kvc, w_kr = w_in[0, :, s1:s2].astype(BF16), w_in[0, :, s2:s3].astype(BF16)
    w1_tail = jnp.concatenate([w_kr, _rot_half(w_kr), jnp.zeros((D_MODEL, IN_COLS_PAD - IN_COLS), BF16)], axis=1)
    w1 = _regroup_w_in(w_in[0], w1_tail)
    w1c = jnp.concatenate([w_kvc, w_kr, w_kr], axis=1)
    p1 = _in_proj(x, norm_pre_mix[0], sh1, sc1, w1, _tile(seq, 1024), IN_BLOCK_N)
    pc = _in_proj(ctx, norm_pre_mix[0], sh1c, sc1c, w1c, _tile(ctx.shape[1], 1024), w1c.shape[1])

    wq = w_uq[0].astype(BF16).reshape(Q_LORA, MLA_HEADS, QK_HEAD)
    wq_ext = jnp.concatenate([wq, _rot_half(wq[:, :, QK_NOPE:])], axis=-1).reshape(Q_LORA, MLA_HEADS * HEAD_PAD)
    wkv = w_ukv[0].astype(BF16).reshape(KV_LORA, MLA_HEADS, QK_NOPE + V_HEAD)
    wkv_re = jnp.concatenate([wkv[:, :, :QK_NOPE].reshape(KV_LORA, -1), wkv[:, :, QK_NOPE:].reshape(KV_LORA, -1)],
                             axis=1)
    tabq, tabk = _rope_tables(seq)
    q, k_lat, v_lat = _mla_latent(p1, q_norm[0], kv_norm[0], wq_ext, wkv_re, tabq, tabk, _tile(seq, 512))
    k_ctx, v_ctx = _mla_ctx(pc, kv_norm[0], wkv_re, _tile(ctx.shape[1], 256))

    attn = _attention(q, k_lat, v_lat, k_ctx, v_ctx, _tile(seq, ATTN_TQ), _tile(seq, ATTN_CHUNK), ATTN_HEADS)

    merged = _merge(p1, attn, pool_w[0].astype(BF16), pool_scale[0], w_pool_up[0].astype(BF16),
                    w_attn_o[0].astype(BF16), _tile(seq, 512))

    wr = jnp.concatenate([router_expert_w[0], router_group_w[0],
                          jnp.zeros((D_MODEL, LANES - N_EXPERTS - N_GROUPS), F32)], axis=1)
    wr_hi = wr.astype(BF16)
    wr_lo = jnp.concatenate([wr_hi, (wr - wr_hi.astype(F32)).astype(BF16)], axis=1)
    br = jnp.concatenate([router_expert_b[0], router_group_b[0],
                          jnp.zeros((LANES - N_EXPERTS - N_GROUPS,), F32)]).reshape(1, LANES)
    x1, hw, route, counts = _out_route(merged, x, w_out[0].astype(BF16), norm_post_mix[0], g1, norm_pre_ffn[0],
                                       sh2, sc2, wr_hi, wr_lo, br, _tile(seq, 512))

    tokens = bsz * seq
    n_tiles_max = 2 * tokens // MOE_TILE + MOE_TILES_MAX_EXTRA
    cnt = counts[0, :N_EXPERTS].astype(jnp.int32)
    tiles_per = (cnt + MOE_TILE - 1) // MOE_TILE
    tile_end = jnp.cumsum(tiles_per)
    seg_start = (tile_end - tiles_per) * MOE_TILE
    n_tiles = tile_end[-1:]
    tile_id = jnp.minimum(jnp.arange(n_tiles_max, dtype=jnp.int32), n_tiles[0] - 1)
    tile_expert = jnp.sum((tile_id[:, None] >= tile_end[None, :]).astype(jnp.int32), axis=1)
    tile_first = jnp.concatenate([jnp.ones((1,), jnp.int32),
                                  (tile_expert[1:] != tile_expert[:-1]).astype(jnp.int32)])
    rt = route.reshape(tokens, LANES)[:, :8].T.astype(jnp.int32)
    experts = jnp.arange(N_EXPERTS, dtype=jnp.int32)[:, None]

    def dest_of(e_lane, r_lane):
        seg = jnp.sum(jnp.where(rt[e_lane][None, :] == experts, seg_start[:, None], 0), axis=0)
        return rt[r_lane] + seg

    dest = (dest_of(ROUTE_E1, ROUTE_R1), dest_of(ROUTE_E2, ROUTE_R2))

    xs = _dispatch(hw.reshape(tokens, D_MODEL), dest, seg_start + cnt, tiles_per * MOE_TILE - cnt,
                   n_tiles, n_tiles_max, _tile(tokens, 1024))
    ys = _moe(xs, tile_expert, tile_first, n_tiles, w_gate[0], w_up[0], w_down[0])
    return _final(x1, route, ys, dest, norm_post_ffn[0], g2, _tile(seq, 256))


def _dest_blocks(dest, tm):
    d1, d2 = dest
    n = d1.shape[0] // tm
    return jnp.concatenate([d1.reshape(n, 1, tm), d2.reshape(n, 1, tm)], axis=2)
```
